```python
import jax, jax.numpy as jnp
from jax import lax
import numpy as np


D_MODEL = 4096
BATCH = 4
SEQ = 2048
DEPTH = 4
DEC_BATCH = 128
DEC_SEQ = 8
PAST_LEN = 8192
PAGE_SIZE = 128

N_MIXERS = 2
N_MLA_LAYERS = (DEPTH + 1) // 2
N_FOX_LAYERS = DEPTH // 2
D_FF = 4 * D_MODEL
RMS_EPS = 1e-6
Q_BLOCK = 128

MLA_HEADS = 32
MLA_Q_LORA = 1024
MLA_KV_LORA = 512
MLA_NOPE = 128
MLA_ROPE = 64
MLA_V = 128
MLA_SCALE = (MLA_NOPE + MLA_ROPE) ** -0.5
ROPE_THETA = 10000.0

FOX_HEADS = 32
FOX_HEAD_DIM = D_MODEL // FOX_HEADS
FOX_KV_HEADS = 2
FOX_GROUP = FOX_HEADS // FOX_KV_HEADS
FOX_SCALE = FOX_HEAD_DIM ** -0.5
FOX_GATE_BIAS_INIT = 2.0

kernel_name = 'mla_fox_interleaved_decoder_step'


def rms_norm(x, g):
    xf = x.astype(jnp.float32)
    y = xf * lax.rsqrt(jnp.mean(xf * xf, axis=-1, keepdims=True) + RMS_EPS)
    return (y * g.astype(jnp.float32)).astype(x.dtype)


def rope(x, pos):
    half = MLA_ROPE // 2
    inv_freq = ROPE_THETA ** (-jnp.arange(half, dtype=jnp.float32) / half)
    ang = pos.astype(jnp.float32)[:, None] * inv_freq[None, :]
    cos = jnp.cos(ang)[None, :, None, :]
    sin = jnp.sin(ang)[None, :, None, :]
    xf = x.astype(jnp.float32)
    x1, x2 = xf[..., :half], xf[..., half:]
    return jnp.concatenate([x1 * cos - x2 * sin, x1 * sin + x2 * cos], axis=-1).astype(x.dtype)


def squared_relu_mlp(h, w_up, w_down):
    return jnp.square(jax.nn.relu(h @ w_up)) @ w_down


def mla_project(h, pos, w_dq, g_q, w_uq, w_dkv, g_kv, w_uk):
    cq = rms_norm(h @ w_dq, g_q)
    q = jnp.einsum('btc,chr->bthr', cq, w_uq)
    q_lat = jnp.einsum('bthn,chn->bthc', q[..., :MLA_NOPE], w_uk)
    q_pe = rope(q[..., MLA_NOPE:], pos)
    kv = h @ w_dkv
    ckv = rms_norm(kv[..., :MLA_KV_LORA], g_kv)
    kpe = rope(kv[:, :, None, MLA_KV_LORA:], pos)[:, :, 0]
    return q_lat, q_pe, ckv, kpe


def mla_prompt_attention(q_lat, q_pe, ckv, kpe, w_uv):
    b, t, h, _ = q_lat.shape
    k_pos = jnp.arange(t)

    def block(i):
        start = i * Q_BLOCK
        ql = lax.dynamic_slice_in_dim(q_lat, start, Q_BLOCK, axis=1)
        qp = lax.dynamic_slice_in_dim(q_pe, start, Q_BLOCK, axis=1)
        s = (jnp.einsum('bqhc,bkc->bhqk', ql, ckv)
             + jnp.einsum('bqhr,bkr->bhqk', qp, kpe)).astype(jnp.float32) * MLA_SCALE
        q_pos = start + jnp.arange(Q_BLOCK)
        s = jnp.where(k_pos[None, :] <= q_pos[:, None], s, -jnp.inf)
        p = jax.nn.softmax(s, axis=-1).astype(ckv.dtype)
        o_lat = jnp.einsum('bhqk,bkc->bqhc', p, ckv)
        return jnp.einsum('bqhc,chv->bqhv', o_lat, w_uv)

    o = lax.map(block, jnp.arange(t // Q_BLOCK))
    return jnp.moveaxis(o, 0, 1).reshape(b, t, h * MLA_V)


def mla_sample_attention(q_lat, q_pe, ckv, kpe, ckv_past, kpe_past, w_uv):
    b, t, h, _ = q_lat.shape
    p_len = ckv_past.shape[1]
    s_past = jnp.einsum('bqhc,bkc->bhqk', q_lat, ckv_past) + jnp.einsum('bqhr,bkr->bhqk', q_pe, kpe_past)
    s_new = jnp.einsum('bqhc,bkc->bhqk', q_lat, ckv) + jnp.einsum('bqhr,bkr->bhqk', q_pe, kpe)
    s = jnp.concatenate([s_past, s_new], axis=-1).astype(jnp.float32) * MLA_SCALE
    mask = jnp.concatenate([jnp.ones((t, p_len), dtype=bool), jnp.tril(jnp.ones((t, t), dtype=bool))], axis=-1)
    s = jnp.where(mask, s, -jnp.inf)
    p = jax.nn.softmax(s, axis=-1).astype(ckv.dtype)
    o_lat = jnp.einsum('bhqk,bkc->bqhc', p[..., :p_len], ckv_past) + jnp.einsum('bhqk,bkc->bqhc', p[..., p_len:], ckv)
    return jnp.einsum('bqhc,chv->bqhv', o_lat, w_uv).reshape(b, t, h * MLA_V)


def fox_project(h, w_q, w_k, w_v, w_f, b_f):
    b, t, _ = h.shape
    q = (h @ w_q).reshape(b, t, FOX_KV_HEADS, FOX_GROUP, FOX_HEAD_DIM)
    k = (h @ w_k).reshape(b, t, FOX_KV_HEADS, FOX_HEAD_DIM)
    v = (h @ w_v).reshape(b, t, FOX_KV_HEADS, FOX_HEAD_DIM)
    logf = jax.nn.log_sigmoid((h @ w_f + b_f).astype(jnp.float32))
    return q, k, v, logf


def heads_first(a):
    b, t, _ = a.shape
    return jnp.transpose(a.reshape(b, t, FOX_KV_HEADS, FOX_GROUP), (0, 2, 3, 1))


def fox_prompt_attention(q, k, v, logf):
    b, t = q.shape[:2]
    c = heads_first(jnp.cumsum(logf, axis=1))
    k_pos = jnp.arange(t)

    def block(i):
        start = i * Q_BLOCK
        qb = lax.dynamic_slice_in_dim(q, start, Q_BLOCK, axis=1)
        cq = lax.dynamic_slice_in_dim(c, start, Q_BLOCK, axis=3)
        s = (jnp.einsum('bqnge,bsne->bngqs', qb, k).astype(jnp.float32) * FOX_SCALE
             + (cq[..., :, None] - c[..., None, :]))
        q_pos = start + jnp.arange(Q_BLOCK)
        s = jnp.where(k_pos[None, :] <= q_pos[:, None], s, -jnp.inf)
        p = jax.nn.softmax(s, axis=-1).astype(v.dtype)
        o = jnp.einsum('bngqs,bsne->bqnge', p, v)
        return o.reshape(b, Q_BLOCK, FOX_HEADS * FOX_HEAD_DIM)

    o = lax.map(block, jnp.arange(t // Q_BLOCK))
    return jnp.moveaxis(o, 0, 1).reshape(b, t, FOX_HEADS * FOX_HEAD_DIM)


def fox_sample_attention(q, k, v, logf, k_past, v_past, logf_past):
    b, t = q.shape[:2]
    p_len = k_past.shape[1]
    lf_past = logf_past.astype(jnp.float32)
    r_past = heads_first(lax.cumsum(lf_past, axis=1, reverse=True) - lf_past)
    c_new = heads_first(jnp.cumsum(logf, axis=1))
    bias = jnp.concatenate([c_new[..., :, None] + r_past[..., None, :],
                            c_new[..., :, None] - c_new[..., None, :]], axis=-1)
    s = jnp.concatenate([jnp.einsum('bqnge,bsne->bngqs', q, k_past),
                         jnp.einsum('bqnge,bsne->bngqs', q, k)], axis=-1).astype(jnp.float32) * FOX_SCALE + bias
    mask = jnp.concatenate([jnp.ones((t, p_len), dtype=bool), jnp.tril(jnp.ones((t, t), dtype=bool))], axis=-1)
    s = jnp.where(mask, s, -jnp.inf)
    p = jax.nn.softmax(s, axis=-1).astype(v.dtype)
    o = jnp.einsum('bngqs,bsne->bqnge', p[..., :p_len], v_past) + jnp.einsum('bngqs,bsne->bqnge', p[..., p_len:], v)
    return o.reshape(b, t, FOX_HEADS * FOX_HEAD_DIM)


def setup_inputs(seed: int = 0) -> dict:
    key = jax.random.key(seed)
    ks = jax.random.split(key, 32)
    f32 = jnp.float32

    def nrm(k, shape, scale):
        return jax.random.normal(k, shape, f32) * scale

    n_pages = PAST_LEN // PAGE_SIZE
    n_used = DEC_BATCH * n_pages
    n_pool = n_used + n_used // 4 + 1
    page_table = jax.random.permutation(ks[0], n_pool)[:n_used].reshape(DEC_BATCH, n_pages).astype(jnp.int32)

    return {
        'x_prompt': nrm(ks[1], (BATCH, SEQ, D_MODEL), 1.0),
        'x_sample': nrm(ks[2], (DEC_BATCH, DEC_SEQ, D_MODEL), 1.0),
        'cache_mla_ckv': nrm(ks[3], (N_MLA_LAYERS, n_pool, PAGE_SIZE, MLA_KV_LORA), 1.0),
        'cache_mla_kpe': nrm(ks[4], (N_MLA_LAYERS, n_pool, PAGE_SIZE, MLA_ROPE), 1.0),
        'cache_fox_k': nrm(ks[5], (N_FOX_LAYERS, n_pool, PAGE_SIZE, FOX_KV_HEADS, FOX_HEAD_DIM), 1.0),
        'cache_fox_v': nrm(ks[6], (N_FOX_LAYERS, n_pool, PAGE_SIZE, FOX_KV_HEADS, FOX_HEAD_DIM), 1.0),
        'cache_fox_logf': jax.nn.log_sigmoid(FOX_GATE_BIAS_INIT + nrm(ks[7], (N_FOX_LAYERS, n_pool, PAGE_SIZE, FOX_HEADS), 1.0)),
        'page_table': page_table,
        'norm_mixer': 1.0 + nrm(ks[8], (DEPTH, D_MODEL), 0.02),
        'norm_mlp': 1.0 + nrm(ks[9], (DEPTH, D_MODEL), 0.02),
        'norm_final': 1.0 + nrm(ks[10], (D_MODEL,), 0.02),
        'mla_w_dq': nrm(ks[11], (N_MLA_LAYERS, D_MODEL, MLA_Q_LORA), D_MODEL ** -0.5),
        'mla_g_q': 1.0 + nrm(ks[12], (N_MLA_LAYERS, MLA_Q_LORA), 0.02),
        'mla_w_uq': nrm(ks[13], (N_MLA_LAYERS, MLA_Q_LORA, MLA_HEADS, MLA_NOPE + MLA_ROPE), MLA_Q_LORA ** -0.5),
        'mla_w_dkv': nrm(ks[14], (N_MLA_LAYERS, D_MODEL, MLA_KV_LORA + MLA_ROPE), D_MODEL ** -0.5),
        'mla_g_kv': 1.0 + nrm(ks[15], (N_MLA_LAYERS, MLA_KV_LORA), 0.02),
        'mla_w_uk': nrm(ks[16], (N_MLA_LAYERS, MLA_KV_LORA, MLA_HEADS, MLA_NOPE), MLA_KV_LORA ** -0.5),
        'mla_w_uv': nrm(ks[17], (N_MLA_LAYERS, MLA_KV_LORA, MLA_HEADS, MLA_V), MLA_KV_LORA ** -0.5),
        'mla_w_o': nrm(ks[18], (N_MLA_LAYERS, MLA_HEADS * MLA_V, D_MODEL), (MLA_HEADS * MLA_V) ** -0.5),
        'fox_w_q': nrm(ks[19], (N_FOX_LAYERS, D_MODEL, FOX_HEADS * FOX_HEAD_DIM), D_MODEL ** -0.5),
        'fox_w_k': nrm(ks[20], (N_FOX_LAYERS, D_MODEL, FOX_KV_HEADS * FOX_HEAD_DIM), D_MODEL ** -0.5),
        'fox_w_v': nrm(ks[21], (N_FOX_LAYERS, D_MODEL, FOX_KV_HEADS * FOX_HEAD_DIM), D_MODEL ** -0.5),
        'fox_w_f': nrm(ks[22], (N_FOX_LAYERS, D_MODEL, FOX_HEADS), D_MODEL ** -0.5),
        'fox_b_f': FOX_GATE_BIAS_INIT + nrm(ks[23], (N_FOX_LAYERS, FOX_HEADS), 0.5),
        'fox_w_o': nrm(ks[24], (N_FOX_LAYERS, FOX_HEADS * FOX_HEAD_DIM, D_MODEL), (FOX_HEADS * FOX_HEAD_DIM) ** -0.5),
        'mlp_w_up': nrm(ks[25], (DEPTH, D_MODEL, D_FF), D_MODEL ** -0.5),
        'mlp_w_down': nrm(ks[26], (DEPTH, D_FF, D_MODEL), D_FF ** -0.5),
    }


def reference(x_prompt, x_sample, cache_mla_ckv, cache_mla_kpe, cache_fox_k, cache_fox_v, cache_fox_logf,
              page_table, norm_mixer, norm_mlp, norm_final,
              mla_w_dq, mla_g_q, mla_w_uq, mla_w_dkv, mla_g_kv, mla_w_uk, mla_w_uv, mla_w_o,
              fox_w_q, fox_w_k, fox_w_v, fox_w_f, fox_b_f, fox_w_o,
              mlp_w_up, mlp_w_down):
    dec_b, n_pages = page_table.shape
    past_len = n_pages * PAGE_SIZE
    pos_prompt = jnp.arange(x_prompt.shape[1])
    pos_sample = past_len + jnp.arange(x_sample.shape[1])

    xp, xs = x_prompt, x_sample
    p_ckv, p_kpe, s_ckv, s_kpe = [], [], [], []
    p_fk, p_fv, p_fl, s_fk, s_fv, s_fl = [], [], [], [], [], []

    for i in range(DEPTH):
        li = i // N_MIXERS
        hp = rms_norm(xp, norm_mixer[i])
        hs = rms_norm(xs, norm_mixer[i])
        if i % N_MIXERS == 0:
            w = (mla_w_dq[li], mla_g_q[li], mla_w_uq[li], mla_w_dkv[li], mla_g_kv[li], mla_w_uk[li])
            w_uv = mla_w_uv[li]
            ql, qp, ckv, kpe = mla_project(hp, pos_prompt, *w)
            ap = mla_prompt_attention(ql, qp, ckv, kpe, w_uv)
            p_ckv.append(ckv)
            p_kpe.append(kpe)
            ql, qp, ckv, kpe = mla_project(hs, pos_sample, *w)
            ckv_past = cache_mla_ckv[li, page_table].reshape(dec_b, past_len, MLA_KV_LORA).astype(xs.dtype)
            kpe_past = cache_mla_kpe[li, page_table].reshape(dec_b, past_len, MLA_ROPE).astype(xs.dtype)
            as_ = mla_sample_attention(ql, qp, ckv, kpe, ckv_past, kpe_past, w_uv)
            s_ckv.append(ckv)
            s_kpe.append(kpe)
            xp = xp + ap @ mla_w_o[li]
            xs = xs + as_ @ mla_w_o[li]
        else:
            w = (fox_w_q[li], fox_w_k[li], fox_w_v[li], fox_w_f[li], fox_b_f[li])
            q, k, v, logf = fox_project(hp, *w)
            ap = fox_prompt_attention(q, k, v, logf)
            p_fk.append(k)
            p_fv.append(v)
            p_fl.append(logf.astype(cache_fox_logf.dtype))
            q, k, v, logf = fox_project(hs, *w)
            k_past = cache_fox_k[li, page_table].reshape(dec_b, past_len, FOX_KV_HEADS, FOX_HEAD_DIM).astype(xs.dtype)
            v_past = cache_fox_v[li, page_table].reshape(dec_b, past_len, FOX_KV_HEADS, FOX_HEAD_DIM).astype(xs.dtype)
            lf_past = cache_fox_logf[li, page_table].reshape(dec_b, past_len, FOX_HEADS)
            as_ = fox_sample_attention(q, k, v, logf, k_past, v_past, lf_past)
            s_fk.append(k)
            s_fv.append(v)
            s_fl.append(logf.astype(cache_fox_logf.dtype))
            xp = xp + ap @ fox_w_o[li]
            xs = xs + as_ @ fox_w_o[li]
        xp = xp + squared_relu_mlp(rms_norm(xp, norm_mlp[i]), mlp_w_up[i], mlp_w_down[i])
        xs = xs + squared_relu_mlp(rms_norm(xs, norm_mlp[i]), mlp_w_up[i], mlp_w_down[i])

    y_prompt = rms_norm(xp, norm_final)
    y_sample = rms_norm(xs, norm_final)
    return (y_prompt, y_sample,
            jnp.stack(p_ckv), jnp.stack(p_kpe), jnp.stack(p_fk), jnp.stack(p_fv), jnp.stack(p_fl),
            jnp.stack(s_ckv), jnp.stack(s_kpe), jnp.stack(s_fk), jnp.stack(s_fv), jnp.stack(s_fl))
```

```python
import functools
import math

import jax
import jax.numpy as jnp
from jax import lax
from jax.experimental import pallas as pl
from jax.experimental.pallas import tpu as pltpu

F32 = jnp.float32
BF16 = jnp.bfloat16

RMS_EPS = 1e-6
ROPE_THETA = 10000.0
NEG_BIG = -1e30

V7X_LANES = 128
V7X_BF16_SUBLANES = 16
V7X_VMEM_LIMIT_BYTES = 56 * 1024 * 1024

NT_DIMS = (((1,), (1,)), ((), ()))

ROW_TILE = 512
N_TILE = 512
Q_TILE = 512
KV_CHUNK = 1024
MLP_ROW_TILE = 512
MLP_F_TILE = 512


def _params(semantics, vmem=V7X_VMEM_LIMIT_BYTES):
    return pltpu.CompilerParams(dimension_semantics=semantics, vmem_limit_bytes=vmem)


def _largest_tile(n, target, multiple):
    best = None
    for t in range(multiple, min(n, target) + 1, multiple):
        if n % t == 0:
            best = t
    assert best is not None, (n, target, multiple)
    return best


def _row_tile(tp, ts, target):
    return _largest_tile(math.gcd(tp, ts), target, V7X_BF16_SUBLANES)


def _rms(x, g, eps=RMS_EPS):
    r = lax.rsqrt(jnp.mean(x * x, axis=-1, keepdims=True) + eps)
    return (x * r) * g


def _log_sigmoid(x):
    return jnp.minimum(x, 0.0) - jnp.log1p(jnp.exp(-jnp.abs(x)))


def _split3(x):
    hi = x.astype(BF16)
    r1 = x - hi.astype(F32)
    mid = r1.astype(BF16)
    lo = (r1 - mid.astype(F32)).astype(BF16)
    return hi, mid, lo


def _dot3(x, m):
    hi, mid, lo = _split3(x)
    out = jnp.dot(lo, m, preferred_element_type=F32)
    out = out + jnp.dot(mid, m, preferred_element_type=F32)
    return out + jnp.dot(hi, m, preferred_element_type=F32)


def _mm_kernel(*refs, n_a, np_tiles, do_norm, has_res, has_rope, scale):
    a_refs = refs[:n_a]
    pos = n_a
    g_ref = None
    if do_norm:
        g_ref = refs[pos]
        pos += 1
    w_ref = refs[pos]
    pos += 1
    res_ref = cos_ref = sin_ref = None
    if has_res:
        res_ref = refs[pos]
        pos += 1
    if has_rope:
        cos_ref, sin_ref = refs[pos], refs[pos + 1]
        pos += 2
    o_ref, h_ref = refs[pos], refs[pos + 1]

    i = pl.program_id(0)

    def fill(a_ref):
        x = a_ref[...].astype(F32)
        if do_norm:
            x = _rms(x, g_ref[...])
        h_ref[...] = x.astype(BF16)

    @pl.when(pl.program_id(1) == 0)
    def _():
        if n_a == 1:
            fill(a_refs[0])
        else:
            @pl.when(i < np_tiles)
            def _():
                fill(a_refs[0])

            @pl.when(i >= np_tiles)
            def _():
                fill(a_refs[1])

    acc = jnp.dot(h_ref[...], w_ref[...], preferred_element_type=F32)
    if has_rope:
        half = acc.shape[1] // 2
        acc = acc[:, :half] * cos_ref[...] + acc[:, half:] * sin_ref[...]
    if scale != 1.0:
        acc = acc * scale
    if has_res:
        acc = res_ref[...] + acc
    o_ref[...] = acc.astype(o_ref.dtype)


def _mm(a, w, *, rows, tm, tn, out_dtype, k_block=0, a2=None, np_tiles=0, g=None, res=None,
        rope=None, scale=1.0, name="mm"):
    k, n = w.shape
    assert rows % tm == 0 and n % tn == 0
    n_out = n // 2 if rope is not None else n
    tn_out = tn // 2 if rope is not None else tn
    grid = (rows // tm, n // tn)

    in_specs, args = [], []
    if a2 is None:
        in_specs.append(pl.BlockSpec((tm, k), lambda i, j: (i, k_block)))
        args.append(a)
    else:
        last = np_tiles - 1
        in_specs.append(pl.BlockSpec((tm, k), lambda i, j: (jnp.minimum(i, last), k_block)))
        in_specs.append(pl.BlockSpec((tm, k), lambda i, j: (jnp.maximum(i - np_tiles, 0), k_block)))
        args += [a, a2]
    if g is not None:
        in_specs.append(pl.BlockSpec((1, k), lambda i, j: (0, 0)))
        args.append(g.reshape(1, k).astype(F32))
    in_specs.append(pl.BlockSpec((k, tn), lambda i, j: (0, j)))
    args.append(w)
    if res is not None:
        in_specs.append(pl.BlockSpec((tm, tn), lambda i, j: (i, j)))
        args.append(res)
    if rope is not None:
        for t in rope:
            in_specs.append(pl.BlockSpec((tm, tn_out), lambda i, j: (i, 0)))
            args.append(t)

    kern = functools.partial(_mm_kernel, n_a=1 if a2 is None else 2, np_tiles=np_tiles,
                             do_norm=g is not None, has_res=res is not None,
                             has_rope=rope is not None, scale=scale)
    return pl.pallas_call(
        kern,
        grid=grid,
        in_specs=in_specs,
        out_specs=pl.BlockSpec((tm, tn_out), lambda i, j: (i, j)),
        out_shape=jax.ShapeDtypeStruct((rows, n_out), out_dtype),
        scratch_shapes=[pltpu.VMEM((tm, k), BF16)],
        compiler_params=_params(("parallel", "arbitrary")),
        name=name,
    )(*args)


def _mlp_kernel(x_ref, g_ref, wu_ref, wd_ref, o_ref, h_ref):
    @pl.when(pl.program_id(1) == 0)
    def _():
        x = x_ref[...]
        h_ref[...] = _rms(x, g_ref[...]).astype(BF16)
        o_ref[...] = x

    u = jnp.dot(h_ref[...], wu_ref[...], preferred_element_type=F32)
    a = jnp.square(jnp.maximum(u, 0.0)).astype(BF16)
    o_ref[...] += jnp.dot(a, wd_ref[...], preferred_element_type=F32)


def _mlp(x, g, w_up, w_down, *, tm, tf):
    t, d = x.shape
    f = w_up.shape[1]
    return pl.pallas_call(
        _mlp_kernel,
        grid=(t // tm, f // tf),
        in_specs=[
            pl.BlockSpec((tm, d), lambda i, j: (i, 0), pipeline_mode=pl.Buffered(1)),
            pl.BlockSpec((1, d), lambda i, j: (0, 0)),
            pl.BlockSpec((d, tf), lambda i, j: (0, j)),
            pl.BlockSpec((tf, d), lambda i, j: (j, 0)),
        ],
        out_specs=pl.BlockSpec((tm, d), lambda i, j: (i, 0)),
        out_shape=jax.ShapeDtypeStruct((t, d), F32),
        scratch_shapes=[pltpu.VMEM((tm, d), BF16)],
        compiler_params=_params(("parallel", "arbitrary")),
        name="mlp",
    )(x, g.reshape(1, d).astype(F32), w_up, w_down)


def _norm_kernel(x_ref, g_ref, o_ref):
    o_ref[...] = _rms(x_ref[...], g_ref[...])


def _final_norm(x, g, *, tm):
    t, d = x.shape
    return pl.pallas_call(
        _norm_kernel,
        grid=(t // tm,),
        in_specs=[pl.BlockSpec((tm, d), lambda i: (i, 0)), pl.BlockSpec((1, d), lambda i: (0, 0))],
        out_specs=pl.BlockSpec((tm, d), lambda i: (i, 0)),
        out_shape=jax.ShapeDtypeStruct((t, d), F32),
        compiler_params=_params(("parallel",)),
        name="final_norm",
    )(x, g.reshape(1, d).astype(F32))


def _kv_post_kernel(p_ref, g_ref, cos_ref, sin_ref, ckv_ref, kpe_ref, *, c):
    ckv_ref[...] = _rms(p_ref[:, :c], g_ref[...])
    w = kpe_ref.shape[1]
    kpe_ref[...] = p_ref[:, c:c + w] * cos_ref[...] + p_ref[:, c + w:c + 2 * w] * sin_ref[...]


def _kv_post(proj, g_kv, cos, sin, *, c, tm):
    t = proj.shape[0]
    wb = c + 2 * V7X_LANES
    return pl.pallas_call(
        functools.partial(_kv_post_kernel, c=c),
        grid=(t // tm,),
        in_specs=[
            pl.BlockSpec((tm, wb), lambda i: (i, 0)),
            pl.BlockSpec((1, c), lambda i: (0, 0)),
            pl.BlockSpec((tm, V7X_LANES), lambda i: (i, 0)),
            pl.BlockSpec((tm, V7X_LANES), lambda i: (i, 0)),
        ],
        out_specs=[pl.BlockSpec((tm, c), lambda i: (i, 0)), pl.BlockSpec((tm, V7X_LANES), lambda i: (i, 0))],
        out_shape=[jax.ShapeDtypeStruct((t, c), F32), jax.ShapeDtypeStruct((t, V7X_LANES), F32)],
        compiler_params=_params(("parallel",)),
        name="mla_kv_post",
    )(proj, g_kv.reshape(1, c).astype(F32), cos, sin)


def _flash_step(qc, kc, vc, carry, mask=None, bias=None):
    m, l, acc = carry
    s = lax.dot_general(qc, kc, NT_DIMS, preferred_element_type=F32)
    if bias is not None:
        s = s + bias
    if mask is not None:
        s = jnp.where(mask, s, NEG_BIG)
    m_new = jnp.maximum(m, jnp.max(s, axis=-1, keepdims=True))
    alpha = jnp.exp(m - m_new)
    p = jnp.exp(s - m_new)
    l = alpha * l + jnp.sum(p, axis=-1, keepdims=True)
    acc = alpha * acc + jnp.dot(p.astype(BF16), vc, preferred_element_type=F32)
    return m_new, l, acc


def _flash_init(rows, dv):
    return (jnp.full((rows, 1), NEG_BIG, F32), jnp.zeros((rows, 1), F32), jnp.zeros((rows, dv), F32))


def _mla_pattn_kernel(qn_ref, qp_ref, k_ref, v_ref, pe_ref, o_ref, kcat_ref, *, tq, dn, dv):
    i = pl.program_id(2)

    @pl.when(i == 0)
    def _():
        pe = pe_ref[...].astype(BF16)
        for hh in range(2):
            kcat_ref[hh, :, 0:dn] = k_ref[:, hh * dn:(hh + 1) * dn]
            kcat_ref[hh, :, dn:dn + V7X_LANES] = pe

    lane = lax.broadcasted_iota(jnp.int32, (tq, V7X_LANES), 1)
    row = lax.broadcasted_iota(jnp.int32, (tq, tq), 0)
    col = lax.broadcasted_iota(jnp.int32, (tq, tq), 1)
    causal = col <= row
    qp = qp_ref[...]
    for hh in range(2):
        keep = (lane >= V7X_LANES // 2) if hh == 1 else (lane < V7X_LANES // 2)
        qc = jnp.concatenate([qn_ref[:, hh * dn:(hh + 1) * dn], jnp.where(keep, qp, jnp.zeros_like(qp))], axis=1)

        def body(j, carry, hh=hh, qc=qc):
            off = pl.multiple_of(j * tq, tq)
            return _flash_step(qc, kcat_ref[hh, pl.ds(off, tq), :], v_ref[pl.ds(off, tq), hh * dv:(hh + 1) * dv], carry)

        carry = lax.fori_loop(0, i, body, _flash_init(tq, dv))
        off = pl.multiple_of(i * tq, tq)
        m, l, acc = _flash_step(qc, kcat_ref[hh, pl.ds(off, tq), :], v_ref[pl.ds(off, tq), hh * dv:(hh + 1) * dv],
                                carry, mask=causal)
        o_ref[:, hh * dv:(hh + 1) * dv] = (acc / l).astype(o_ref.dtype)


def _mla_prompt_attn(q_nope, q_pe, knv, kpe_dup, *, b, s, h, dn, dv, tq):
    nq = s // tq
    hp = h // 2
    v_off = (h * dn) // (2 * dv)
    return pl.pallas_call(
        functools.partial(_mla_pattn_kernel, tq=tq, dn=dn, dv=dv),
        grid=(b, hp, nq),
        in_specs=[
            pl.BlockSpec((tq, 2 * dn), lambda bb, p, i: (bb * nq + i, p)),
            pl.BlockSpec((tq, V7X_LANES), lambda bb, p, i: (bb * nq + i, p)),
            pl.BlockSpec((s, 2 * dn), lambda bb, p, i: (bb, p)),
            pl.BlockSpec((s, 2 * dv), lambda bb, p, i: (bb, v_off + p)),
            pl.BlockSpec((s, V7X_LANES), lambda bb, p, i: (bb, 0)),
        ],
        out_specs=pl.BlockSpec((tq, 2 * dv), lambda bb, p, i: (bb * nq + i, p)),
        out_shape=jax.ShapeDtypeStruct((b * s, h * dv), BF16),
        scratch_shapes=[pltpu.VMEM((2, s, dn + V7X_LANES), BF16)],
        compiler_params=_params(("parallel", "parallel", "arbitrary")),
        name="mla_prompt_attn",
    )(q_nope, q_pe, knv, knv, kpe_dup)


def _head_nt_kernel(a_ref, w_ref, o_ref):
    o_ref[0] = lax.dot_general(a_ref[...], w_ref[...], NT_DIMS, preferred_element_type=F32)


def _head_absorb(q_nope, w_uk_flat, *, h, dn, ts, row_block):
    c = w_uk_flat.shape[0]
    return pl.pallas_call(
        _head_nt_kernel,
        grid=(h,),
        in_specs=[pl.BlockSpec((ts, dn), lambda hh: (row_block, hh)), pl.BlockSpec((c, dn), lambda hh: (0, hh))],
        out_specs=pl.BlockSpec((1, ts, c), lambda hh: (hh, 0, 0)),
        out_shape=jax.ShapeDtypeStruct((h, ts, c), F32),
        compiler_params=_params(("parallel",)),
        name="mla_absorb_q",
    )(q_nope, w_uk_flat)


def _head_nn_kernel(a_ref, w_ref, o_ref):
    o_ref[...] = jnp.dot(a_ref[0].astype(BF16), w_ref[...], preferred_element_type=F32).astype(o_ref.dtype)


def _head_value(o_lat, w_uv_flat, *, h, dv):
    _, ts, c = o_lat.shape
    return pl.pallas_call(
        _head_nn_kernel,
        grid=(h,),
        in_specs=[pl.BlockSpec((1, ts, c), lambda hh: (hh, 0, 0)), pl.BlockSpec((c, dv), lambda hh: (0, hh))],
        out_specs=pl.BlockSpec((ts, dv), lambda hh: (0, hh)),
        out_shape=jax.ShapeDtypeStruct((ts, h * dv), BF16),
        compiler_params=_params(("parallel",)),
        name="mla_value_up",
    )(o_lat, w_uv_flat)


def _page_copy(stream, pt_ref, base, p, slot):
    hbm, layer, buf, sem, extent, axis = stream
    pg = pt_ref[base + p]
    span = pl.ds(pl.multiple_of(p * extent, extent), extent)
    dst = buf.at[slot, span, :] if axis == 0 else buf.at[slot, :, span]
    return pltpu.make_async_copy(hbm.at[layer, pg], dst, sem.at[slot])


def _pages_start(streams, pt_ref, bb, slot, n_pages):
    def body(p, c):
        for st in streams:
            _page_copy(st, pt_ref, bb * n_pages, p, slot).start()
        return c
    lax.fori_loop(0, n_pages, body, 0)


def _pages_wait(streams, pt_ref, bb, slot, n_pages):
    def body(p, c):
        for st in streams:
            _page_copy(st, pt_ref, bb * n_pages, p, slot).wait()
        return c
    lax.fori_loop(0, n_pages, body, 0)


def _paged_prologue(streams, pt_ref, n_pages):
    b = pl.program_id(0)
    slot = lax.rem(b, 2)

    @pl.when(b == 0)
    def _():
        _pages_start(streams, pt_ref, b, slot, n_pages)

    @pl.when(b + 1 < pl.num_programs(0))
    def _():
        _pages_start(streams, pt_ref, b + 1, 1 - slot, n_pages)

    _pages_wait(streams, pt_ref, b, slot, n_pages)
    return slot


def _pad_rows(x, rows):
    return jnp.concatenate([x, jnp.zeros((rows - x.shape[0], x.shape[1]), x.dtype)], axis=0)


def _mla_dec_kernel(pt_ref, ql_ref, qp_ref, cn_ref, pn_ref, ckv_hbm, kpe_hbm, o_ref, kbuf, pbuf, ksem, psem,
                    *, layer, n_pages, page, tk, dr):
    h, t, c = ql_ref.shape
    rows = h * t
    streams = ((ckv_hbm, layer, kbuf, ksem, page, 0), (kpe_hbm, layer, pbuf, psem, page, 1))
    slot = _paged_prologue(streams, pt_ref, n_pages)

    ql = ql_ref[...].reshape(rows, c).astype(BF16)
    qp = qp_ref[...].reshape(rows, dr).astype(BF16)

    def step(kc, s_pe, carry, mask=None):
        m, l, acc = carry
        s = lax.dot_general(ql, kc, NT_DIMS, preferred_element_type=F32) + s_pe
        if mask is not None:
            s = jnp.where(mask, s, NEG_BIG)
        m_new = jnp.maximum(m, jnp.max(s, axis=-1, keepdims=True))
        alpha = jnp.exp(m - m_new)
        p = jnp.exp(s - m_new)
        l = alpha * l + jnp.sum(p, axis=-1, keepdims=True)
        acc = alpha * acc + jnp.dot(p.astype(BF16), kc, preferred_element_type=F32)
        return m_new, l, acc

    def body(j, carry):
        off = pl.multiple_of(j * tk, tk)
        kc = kbuf[slot, pl.ds(off, tk), :].astype(BF16)
        pct = pbuf[slot, :, pl.ds(off, tk)].astype(BF16)
        return step(kc, jnp.dot(qp, pct, preferred_element_type=F32), carry)

    carry = lax.fori_loop(0, (n_pages * page) // tk, body, _flash_init(rows, c))

    kn = _pad_rows(cn_ref[...], V7X_LANES).astype(BF16)
    pn = _pad_rows(pn_ref[:, :dr], V7X_LANES).astype(BF16)
    qpos = lax.rem(lax.broadcasted_iota(jnp.int32, (rows, V7X_LANES), 0), t)
    kpos = lax.broadcasted_iota(jnp.int32, (rows, V7X_LANES), 1)
    s_pe = lax.dot_general(qp, pn, NT_DIMS, preferred_element_type=F32)
    m, l, acc = step(kn, s_pe, carry, mask=kpos <= qpos)
    o_ref[...] = (acc / l).reshape(h, t, c)


def _mla_decode(page_table, q_lat, q_pe_s, ckv, kpe_dup, cache_ckv, cache_kpe, *, layer, tp, tk):
    db, n_pages = page_table.shape
    h, ts, c = q_lat.shape
    t = ts // db
    dr = q_pe_s.shape[2]
    page = cache_ckv.shape[2]
    new_blk = tp // t
    grid_spec = pltpu.PrefetchScalarGridSpec(
        num_scalar_prefetch=1,
        grid=(db,),
        in_specs=[
            pl.BlockSpec((h, t, c), lambda b, pt: (0, b, 0)),
            pl.BlockSpec((h, t, dr), lambda b, pt: (0, b, 0)),
            pl.BlockSpec((t, c), lambda b, pt: (new_blk + b, 0)),
            pl.BlockSpec((t, V7X_LANES), lambda b, pt: (new_blk + b, 0)),
            pl.BlockSpec(memory_space=pl.ANY),
            pl.BlockSpec(memory_space=pl.ANY),
        ],
        out_specs=pl.BlockSpec((h, t, c), lambda b, pt: (0, b, 0)),
        scratch_shapes=[
            pltpu.VMEM((2, n_pages * page, c), F32),
            pltpu.VMEM((2, dr, n_pages * page), F32),
            pltpu.SemaphoreType.DMA((2,)),
            pltpu.SemaphoreType.DMA((2,)),
        ],
    )
    return pl.pallas_call(
        functools.partial(_mla_dec_kernel, layer=layer, n_pages=n_pages, page=page, tk=tk, dr=dr),
        grid_spec=grid_spec,
        out_shape=jax.ShapeDtypeStruct((h, ts, c), F32),
        compiler_params=_params(("arbitrary",)),
        name="mla_decode",
    )(page_table.reshape(-1), q_lat, q_pe_s, ckv, kpe_dup, cache_ckv, cache_kpe)


def _fox_kvz_kernel(x_ref, g_ref, w_ref, wt_ref, b_ref, bt_ref, o_ref, ot_ref, *, kvw):
    hb = _rms(x_ref[...], g_ref[...]).astype(BF16)
    acc = jnp.dot(hb, w_ref[...], preferred_element_type=F32)
    o_ref[:, :kvw] = acc[:, :kvw]
    o_ref[:, kvw:] = _log_sigmoid(acc[:, kvw:] + b_ref[...])
    zt = lax.dot_general(wt_ref[...], hb, NT_DIMS, preferred_element_type=F32)
    ot_ref[...] = _log_sigmoid(zt + bt_ref[...])


def _fox_kvz(x, g, w_kvz, w_ft, b_row, b_col, *, kvw, tm):
    t, d = x.shape
    n = w_kvz.shape[1]
    hh = w_ft.shape[0]
    return pl.pallas_call(
        functools.partial(_fox_kvz_kernel, kvw=kvw),
        grid=(t // tm,),
        in_specs=[
            pl.BlockSpec((tm, d), lambda i: (i, 0)),
            pl.BlockSpec((1, d), lambda i: (0, 0)),
            pl.BlockSpec((d, n), lambda i: (0, 0)),
            pl.BlockSpec((hh, d), lambda i: (0, 0)),
            pl.BlockSpec((1, n - kvw), lambda i: (0, 0)),
            pl.BlockSpec((hh, 1), lambda i: (0, 0)),
        ],
        out_specs=[pl.BlockSpec((tm, n), lambda i: (i, 0)), pl.BlockSpec((hh, tm), lambda i: (0, i))],
        out_shape=[jax.ShapeDtypeStruct((t, n), F32), jax.ShapeDtypeStruct((hh, t), F32)],
        compiler_params=_params(("parallel",)),
        name="fox_kvz",
    )(x, g.reshape(1, d).astype(F32), w_kvz, w_ft, b_row, b_col)


def _tri(n, kind, seg=0):
    j = lax.broadcasted_iota(jnp.int32, (n, n), 0)
    t = lax.broadcasted_iota(jnp.int32, (n, n), 1)
    if kind == "incl":
        m = j <= t
    elif kind == "suffix":
        m = j > t
    if seg:
        m = jnp.logical_and(m, (j // seg) == (t // seg))
    return jnp.where(m, 1.0, 0.0).astype(BF16)


def _cum_kernel(x_ref, o_ref, *, blk, seg):
    hh, n = x_ref.shape
    tri = _tri(blk, "incl", seg)
    carry = jnp.zeros((hh, 1), F32)
    for k in range(n // blk):
        x = x_ref[:, k * blk:(k + 1) * blk]
        o_ref[:, k * blk:(k + 1) * blk] = _dot3(x, tri) + carry
        if not seg:
            carry = carry + jnp.sum(x, axis=-1, keepdims=True)


def _fox_cum(lft, *, width, n_blocks, block0, seg, blk):
    hh = lft.shape[0]
    return pl.pallas_call(
        functools.partial(_cum_kernel, blk=blk, seg=seg),
        grid=(n_blocks,),
        in_specs=[pl.BlockSpec((hh, width), lambda i: (0, block0 + i))],
        out_specs=pl.BlockSpec((hh, width), lambda i: (0, i)),
        out_shape=jax.ShapeDtypeStruct((hh, width * n_blocks), F32),
        compiler_params=_params(("parallel",)),
        name="fox_cumsum",
    )(lft)


def _fox_pattn_kernel(q_ref, k_ref, v_ref, c_ref, o_ref, kb_ref, vb_ref, *, tq, dh, group):
    i = pl.program_id(2)

    @pl.when(i == 0)
    def _():
        kb_ref[...] = k_ref[...].astype(BF16)
        vb_ref[...] = v_ref[...].astype(BF16)

    row = lax.broadcasted_iota(jnp.int32, (tq, tq), 0)
    col = lax.broadcasted_iota(jnp.int32, (tq, tq), 1)
    causal = col <= row

    def head(gi, _):
        lo = pl.multiple_of(gi * dh, dh)
        qc = q_ref[:, pl.ds(lo, dh)]

        def body(j, carry):
            off = pl.multiple_of(j * tq, tq)
            bias = -c_ref[pl.ds(gi, 1), pl.ds(off, tq)]
            return _flash_step(qc, kb_ref[pl.ds(off, tq), :], vb_ref[pl.ds(off, tq), :], carry, bias=bias)

        carry = lax.fori_loop(0, i, body, _flash_init(tq, dh))
        off = pl.multiple_of(i * tq, tq)
        bias = -c_ref[pl.ds(gi, 1), pl.ds(off, tq)]
        m, l, acc = _flash_step(qc, kb_ref[pl.ds(off, tq), :], vb_ref[pl.ds(off, tq), :], carry, mask=causal, bias=bias)
        o_ref[:, pl.ds(lo, dh)] = (acc / l).astype(o_ref.dtype)
        return 0

    lax.fori_loop(0, group, head, 0)


def _fox_prompt_attn(q, kvz, ct, *, b, s, nkv, group, dh, tq):
    nq = s // tq
    return pl.pallas_call(
        functools.partial(_fox_pattn_kernel, tq=tq, dh=dh, group=group),
        grid=(b, nkv, nq),
        in_specs=[
            pl.BlockSpec((tq, group * dh), lambda bb, n, i: (bb * nq + i, n)),
            pl.BlockSpec((s, dh), lambda bb, n, i: (bb, n)),
            pl.BlockSpec((s, dh), lambda bb, n, i: (bb, nkv + n)),
            pl.BlockSpec((group, s), lambda bb, n, i: (n, bb)),
        ],
        out_specs=pl.BlockSpec((tq, group * dh), lambda bb, n, i: (bb * nq + i, n)),
        out_shape=jax.ShapeDtypeStruct((b * s, nkv * group * dh), BF16),
        scratch_shapes=[pltpu.VMEM((s, dh), BF16), pltpu.VMEM((s, dh), BF16)],
        compiler_params=_params(("parallel", "parallel", "arbitrary")),
        name="fox_prompt_attn",
    )(q, kvz, kvz, ct)


def _fox_rpast_kernel(pt_ref, lf_hbm, o_ref, lbuf, sem, *, layer, n_pages, page, blk):
    hh = o_ref.shape[1]
    streams = ((lf_hbm, layer, lbuf, sem, page, 1),)
    slot = _paged_prologue(streams, pt_ref, n_pages)

    n = n_pages * page
    nblk = n // blk
    lt = jnp.concatenate([lbuf[slot, :, k * blk:(k + 1) * blk] for k in range(nblk)], axis=0)
    r = _dot3(lt, _tri(blk, "suffix"))
    tot = jnp.sum(lt, axis=-1, keepdims=True)
    carry = jnp.zeros((hh, 1), F32)
    for k in range(nblk - 1, -1, -1):
        o_ref[0, :, k * blk:(k + 1) * blk] = r[k * hh:(k + 1) * hh, :] + carry
        carry = carry + tot[k * hh:(k + 1) * hh, :]


def _fox_rpast(page_table, cache_logf, *, layer, blk):
    db, n_pages = page_table.shape
    hh, page = cache_logf.shape[2], cache_logf.shape[3]
    n = n_pages * page
    grid_spec = pltpu.PrefetchScalarGridSpec(
        num_scalar_prefetch=1,
        grid=(db,),
        in_specs=[pl.BlockSpec(memory_space=pl.ANY)],
        out_specs=pl.BlockSpec((1, hh, n), lambda b, pt: (b, 0, 0)),
        scratch_shapes=[pltpu.VMEM((2, hh, n), F32), pltpu.SemaphoreType.DMA((2,))],
    )
    return pl.pallas_call(
        functools.partial(_fox_rpast_kernel, layer=layer, n_pages=n_pages, page=page, blk=blk),
        grid_spec=grid_spec,
        out_shape=jax.ShapeDtypeStruct((db, hh, n), F32),
        compiler_params=_params(("arbitrary",)),
        name="fox_rpast",
    )(page_table.reshape(-1), cache_logf)


def _rep_rows(x, t):
    g, n = x.shape
    return jnp.concatenate([jnp.broadcast_to(x[i:i + 1, :], (t, n)) for i in range(g)], axis=0)


def _fox_dec_kernel(pt_ref, q_ref, kvn_ref, r_ref, cn_ref, k_hbm, v_hbm, o_ref, kbuf, vbuf, ksem, vsem,
                    *, layer, n_pages, page, tk, nkv, group, dh):
    t = q_ref.shape[0]
    rows = group * t
    streams = ((k_hbm, layer, kbuf, ksem, page * nkv, 0), (v_hbm, layer, vbuf, vsem, page * nkv, 0))
    slot = _paged_prologue(streams, pt_ref, n_pages)

    qpos = lax.rem(lax.broadcasted_iota(jnp.int32, (rows, V7X_LANES), 0), t)
    kpos = lax.broadcasted_iota(jnp.int32, (rows, V7X_LANES), 1)
    new_mask = kpos <= qpos

    for n in range(nkv):
        qc = jnp.concatenate([q_ref[:, (n * group + g) * dh:(n * group + g + 1) * dh] for g in range(group)],
                             axis=0).astype(BF16)

        def body(j, carry, n=n, qc=qc):
            off = pl.multiple_of(j * tk, tk)
            kc = kbuf[slot, pl.ds(off * nkv + n, tk, stride=nkv), :].astype(BF16)
            vc = vbuf[slot, pl.ds(off * nkv + n, tk, stride=nkv), :].astype(BF16)
            bias = _rep_rows(r_ref[0, n * group:(n + 1) * group, pl.ds(off, tk)], t)
            return _flash_step(qc, kc, vc, carry, bias=bias)

        carry = lax.fori_loop(0, (n_pages * page) // tk, body, _flash_init(rows, dh))

        kn = _pad_rows(kvn_ref[:, n * dh:(n + 1) * dh], V7X_LANES).astype(BF16)
        vn = _pad_rows(kvn_ref[:, (nkv + n) * dh:(nkv + n + 1) * dh], V7X_LANES).astype(BF16)
        bias = _rep_rows(-cn_ref[0, n * group:(n + 1) * group, :], t)
        m, l, acc = _flash_step(qc, kn, vn, carry, mask=new_mask, bias=bias)
        o = acc / l
        for g in range(group):
            o_ref[:, (n * group + g) * dh:(n * group + g + 1) * dh] = o[g * t:(g + 1) * t, :]


def _fox_decode(page_table, q_s, kvz, r_t, cn_b, cache_k, cache_v, *, layer, tp, tk, nkv, group, dh):
    db, n_pages = page_table.shape
    ts, hd = q_s.shape
    t = ts // db
    page = cache_k.shape[2] // nkv
    n = n_pages * page
    new_blk = tp // t
    hh = nkv * group
    grid_spec = pltpu.PrefetchScalarGridSpec(
        num_scalar_prefetch=1,
        grid=(db,),
        in_specs=[
            pl.BlockSpec((t, hd), lambda b, pt: (b, 0)),
            pl.BlockSpec((t, kvz.shape[1]), lambda b, pt: (new_blk + b, 0)),
            pl.BlockSpec((1, hh, n), lambda b, pt: (b, 0, 0)),
            pl.BlockSpec((1, hh, V7X_LANES), lambda b, pt: (b, 0, 0)),
            pl.BlockSpec(memory_space=pl.ANY),
            pl.BlockSpec(memory_space=pl.ANY),
        ],
        out_specs=pl.BlockSpec((t, hd), lambda b, pt: (b, 0)),
        scratch_shapes=[
            pltpu.VMEM((2, n * nkv, dh), F32),
            pltpu.VMEM((2, n * nkv, dh), F32),
            pltpu.SemaphoreType.DMA((2,)),
            pltpu.SemaphoreType.DMA((2,)),
        ],
    )
    return pl.pallas_call(
        functools.partial(_fox_dec_kernel, layer=layer, n_pages=n_pages, page=page, tk=tk, nkv=nkv, group=group, dh=dh),
        grid_spec=grid_spec,
        out_shape=jax.ShapeDtypeStruct((ts, hd), F32),
        compiler_params=_params(("arbitrary",)),
        name="fox_decode",
    )(page_table.reshape(-1), q_s, kvz, r_t, cn_b, cache_k, cache_v)


def _rot_cols(w):
    half = w.shape[-1] // 2
    return jnp.concatenate([-w[..., half:], w[..., :half]], axis=-1)


def _mla_weights(w_dq, w_uq, w_dkv, w_uk, w_uv, *, c, dn, dr, pe_tile):
    d, ql = w_dq.shape
    h = w_uq.shape[1]
    w_c, w_pe = w_dkv[:, :c], w_dkv[:, c:]
    w_rot = _rot_cols(w_pe)
    pad = jnp.zeros((d, ql - (c + 4 * dr) % ql), w_dq.dtype) if (c + 4 * dr) % ql else jnp.zeros((d, 0), w_dq.dtype)
    w1 = jnp.concatenate([w_c, w_pe, w_pe, w_rot, w_rot, pad, w_dq], axis=1).astype(BF16)
    wq_nope = w_uq[:, :, :dn].reshape(ql, h * dn).astype(BF16)
    pe = w_uq[:, :, dn:]
    hpt = pe_tile // dr
    pe_t = pe.reshape(ql, h // hpt, hpt * dr)
    rot_t = _rot_cols(pe).reshape(ql, h // hpt, hpt * dr)
    wq_pe = jnp.concatenate([pe_t, rot_t], axis=2).reshape(ql, 2 * h * dr).astype(BF16)
    w_uk_flat = w_uk.reshape(c, h * dn).astype(BF16)
    w_uv_flat = w_uv.reshape(c, -1).astype(BF16)
    w_kv_up = jnp.concatenate([w_uk_flat, w_uv_flat], axis=1)
    return w1, wq_nope, wq_pe, w_uk_flat, w_uv_flat, w_kv_up


def kernel(x_prompt, x_sample, cache_mla_ckv, cache_mla_kpe, cache_fox_k, cache_fox_v, cache_fox_logf, page_table, norm_mixer, norm_mlp, norm_final, mla_w_dq, mla_g_q, mla_w_uq, mla_w_dkv, mla_g_kv, mla_w_uk, mla_w_uv, mla_w_o, fox_w_q, fox_w_k, fox_w_v, fox_w_f, fox_b_f, fox_w_o, mlp_w_up, mlp_w_down):
    b, s, d = x_prompt.shape
    db, ds, _ = x_sample.shape
    depth = norm_mixer.shape[0]
    n_pages = page_table.shape[1]
    page = cache_mla_ckv.shape[2]
    past_len = n_pages * page
    tp, ts = b * s, db * ds
    t = tp + ts

    ql = mla_w_dq.shape[2]
    c = mla_w_uk.shape[1]
    h = mla_w_uk.shape[2]
    dn = mla_w_uk.shape[3]
    dv = mla_w_uv.shape[3]
    dr = mla_w_uq.shape[3] - dn
    mla_scale = float(dn + dr) ** -0.5

    fh = fox_w_f.shape[2]
    nkv, dh = cache_fox_k.shape[3], cache_fox_k.shape[4]
    group = fh // nkv
    fox_scale = float(dh) ** -0.5

    tm = _row_tile(tp, ts, ROW_TILE)
    tn = N_TILE
    tq = _largest_tile(s, Q_TILE, V7X_LANES)
    tk = _largest_tile(past_len, KV_CHUNK, page)
    tm_mlp = _largest_tile(t, MLP_ROW_TILE, V7X_BF16_SUBLANES)
    tf = _largest_tile(mlp_w_up.shape[2], MLP_F_TILE, V7X_LANES)
    pe_tile = 4 * dr
    np_tiles = tp // tm

    x = jnp.concatenate([x_prompt.reshape(tp, d), x_sample.reshape(ts, d)], axis=0)

    half = dr // 2
    inv_freq = ROPE_THETA ** (-jnp.arange(half, dtype=F32) / half)
    pos = jnp.concatenate([jnp.tile(jnp.arange(s), b), jnp.tile(past_len + jnp.arange(ds), db)])
    ang = pos.astype(F32)[:, None] * inv_freq[None, :]
    cos_t = jnp.tile(jnp.cos(ang), (1, pe_tile // half))
    sin_t = jnp.tile(jnp.sin(ang), (1, pe_tile // half))

    cache_k2 = cache_fox_k.reshape(cache_fox_k.shape[:2] + (page * nkv, dh))
    cache_v2 = cache_fox_v.reshape(cache_fox_v.shape[:2] + (page * nkv, dh))
    cache_kpe_t = jnp.swapaxes(cache_mla_kpe, 2, 3)
    cache_logf_t = jnp.swapaxes(cache_fox_logf, 2, 3)

    outs = {k: [] for k in ("ckv", "kpe", "fk", "fv", "fl")}

    for i in range(depth):
        li = i // 2
        if i % 2 == 0:
            w1, wq_nope, wq_pe, w_uk_flat, w_uv_flat, w_kv_up = _mla_weights(
                mla_w_dq[li], mla_w_uq[li], mla_w_dkv[li], mla_w_uk[li], mla_w_uv[li], c=c, dn=dn, dr=dr, pe_tile=pe_tile)
            w_o = mla_w_o[li].astype(BF16)
            kvw = w1.shape[1] - ql
            proj = _mm(x, w1, rows=t, tm=tm, tn=tn, out_dtype=F32, g=norm_mixer[i], name="mla_down")
            ckv, kpe_dup = _kv_post(proj, mla_g_kv[li], cos_t, sin_t, c=c, tm=tm)
            outs["ckv"].append(ckv)
            outs["kpe"].append(kpe_dup[:, :dr])
            q_nope = _mm(proj, wq_nope, rows=t, tm=tm, tn=tn, out_dtype=BF16, k_block=kvw // ql, g=mla_g_q[li],
                         scale=mla_scale, name="mla_q_nope")
            q_pe = _mm(proj, wq_pe, rows=t, tm=tm, tn=2 * pe_tile, out_dtype=BF16, k_block=kvw // ql, g=mla_g_q[li],
                       rope=(cos_t, sin_t), scale=mla_scale, name="mla_q_pe")
            knv = _mm(ckv, w_kv_up, rows=tp, tm=tm, tn=tn, out_dtype=BF16, name="mla_kv_up")
            attn_p = _mla_prompt_attn(q_nope, q_pe, knv, kpe_dup, b=b, s=s, h=h, dn=dn, dv=dv, tq=tq)
            q_lat = _head_absorb(q_nope, w_uk_flat, h=h, dn=dn, ts=ts, row_block=tp // ts)
            q_pe_s = q_pe[tp:].reshape(ts, h, dr).transpose(1, 0, 2).astype(F32)
            o_lat = _mla_decode(page_table, q_lat, q_pe_s, ckv, kpe_dup, cache_mla_ckv, cache_kpe_t,
                                layer=li, tp=tp, tk=tk)
            attn_s = _head_value(o_lat, w_uv_flat, h=h, dv=dv)
        else:
            w_q = fox_w_q[li].astype(BF16)
            w_o = fox_w_o[li].astype(BF16)
            zpad = V7X_LANES - fh
            w_kvz = jnp.concatenate([fox_w_k[li], fox_w_v[li], fox_w_f[li], jnp.zeros((d, zpad), F32)], axis=1).astype(BF16)
            w_ft = fox_w_f[li].T.astype(BF16)
            b_row = jnp.concatenate([fox_b_f[li], jnp.zeros((zpad,), F32)]).reshape(1, V7X_LANES)
            b_col = fox_b_f[li].reshape(fh, 1)
            kvw = 2 * nkv * dh
            q = _mm(x, w_q, rows=t, tm=tm, tn=tn, out_dtype=BF16, g=norm_mixer[i], scale=fox_scale, name="fox_q")
            kvz, lft = _fox_kvz(x, norm_mixer[i], w_kvz, w_ft, b_row, b_col, kvw=kvw, tm=tm)
            outs["fk"].append(kvz[:, :nkv * dh])
            outs["fv"].append(kvz[:, nkv * dh:kvw])
            outs["fl"].append(kvz[:, kvw:kvw + fh])
            blk = 2 * V7X_LANES
            ct = _fox_cum(lft, width=s, n_blocks=b, block0=0, seg=0, blk=blk)
            attn_p = _fox_prompt_attn(q, kvz, ct, b=b, s=s, nkv=nkv, group=group, dh=dh, tq=tq)
            wn = _largest_tile(ts, blk, math.lcm(ds, V7X_LANES))
            cnt = _fox_cum(lft, width=wn, n_blocks=ts // wn, block0=tp // wn, seg=ds, blk=wn)
            cn_b = cnt.reshape(fh, db, ds).transpose(1, 0, 2)
            cn_b = jnp.concatenate([cn_b, jnp.zeros((db, fh, V7X_LANES - ds), F32)], axis=2)
            r_t = _fox_rpast(page_table, cache_logf_t, layer=li, blk=blk)
            q_s = q[tp:].astype(F32)
            attn_s = _fox_decode(page_table, q_s, kvz, r_t, cn_b, cache_k2, cache_v2, layer=li, tp=tp, tk=tk,
                                 nkv=nkv, group=group, dh=dh)
        x = _mm(attn_p, w_o, rows=t, tm=tm, tn=tn, out_dtype=F32, a2=attn_s, np_tiles=np_tiles, res=x, name="attn_out")
        x = _mlp(x, norm_mlp[i], mlp_w_up[i].astype(BF16), mlp_w_down[i].astype(BF16), tm=tm_mlp, tf=tf)

    y = _final_norm(x, norm_final, tm=tm)

    def split(rows_list, tail):
        st = jnp.stack(rows_list)
        n_l = st.shape[0]
        return st[:, :tp].reshape((n_l, b, s) + tail), st[:, tp:].reshape((n_l, db, ds) + tail)

    p_ckv, s_ckv = split(outs["ckv"], (c,))
    p_kpe, s_kpe = split(outs["kpe"], (dr,))
    p_fk, s_fk = split(outs["fk"], (nkv, dh))
    p_fv, s_fv = split(outs["fv"], (nkv, dh))
    p_fl, s_fl = split(outs["fl"], (fh,))
    return (y[:tp].reshape(b, s, d), y[tp:].reshape(db, ds, d),
            p_ckv, p_kpe, p_fk, p_fv, p_fl, s_ckv, s_kpe, s_fk, s_fv, s_fl)
```

```python
import functools
import math

import jax
import jax.numpy as jnp
from jax import lax
from jax.experimental import pallas as pl
from jax.experimental.pallas import tpu as pltpu

F32 = jnp.float32
BF16 = jnp.bfloat16

RMS_EPS = 1e-6
ROPE_THETA = 10000.0
NEG_BIG = -1e30

V7X_LANES = 128
V7X_BF16_SUBLANES = 16
V7X_VMEM_LIMIT_BYTES = 56 * 1024 * 1024

NT_DIMS = (((1,), (1,)), ((), ()))

ROW_TILE = 512
N_TILE = 512
Q_TILE = 512
ATTN_HEADS_PER_STEP = 4
MLA_KV_CHUNK = 2048
FOX_KV_CHUNK = 4096
MLP_ROW_TILE = 512
MLP_F_TILE = 512
PAGE_ISSUE_UNROLL = 8
N_TILE_BYTES = 8 * 1024 * 1024
N_TILE_MAX = 2048

LOG2E = 1.4426950408889634


def _params(semantics, vmem=V7X_VMEM_LIMIT_BYTES):
    return pltpu.CompilerParams(dimension_semantics=semantics, vmem_limit_bytes=vmem)


def _largest_tile(n, target, multiple):
    best = None
    for t in range(multiple, min(n, target) + 1, multiple):
        if n % t == 0:
            best = t
    assert best is not None, (n, target, multiple)
    return best


def _row_tile(tp, ts, target):
    return _largest_tile(math.gcd(tp, ts), target, V7X_BF16_SUBLANES)


def _rms(x, g, eps=RMS_EPS):
    r = lax.rsqrt(jnp.mean(x * x, axis=-1, keepdims=True) + eps)
    return (x * r) * g


def _log_sigmoid(x):
    return jnp.minimum(x, 0.0) - jnp.log1p(jnp.exp(-jnp.abs(x)))


def _split3(x):
    hi = x.astype(BF16)
    r1 = x - hi.astype(F32)
    mid = r1.astype(BF16)
    lo = (r1 - mid.astype(F32)).astype(BF16)
    return hi, mid, lo


def _dot3(x, m):
    hi, mid, lo = _split3(x)
    out = jnp.dot(lo, m, preferred_element_type=F32)
    out = out + jnp.dot(mid, m, preferred_element_type=F32)
    return out + jnp.dot(hi, m, preferred_element_type=F32)


def _mm_kernel(*refs, n_a, np_tiles, do_norm, has_res, has_rope, scale):
    a_refs = refs[:n_a]
    pos = n_a
    g_ref = None
    if do_norm:
        g_ref = refs[pos]
        pos += 1
    w_ref = refs[pos]
    pos += 1
    res_ref = cos_ref = sin_ref = None
    if has_res:
        res_ref = refs[pos]
        pos += 1
    if has_rope:
        cos_ref, sin_ref = refs[pos], refs[pos + 1]
        pos += 2
    o_ref, h_ref = refs[pos], refs[pos + 1]

    i = pl.program_id(0)

    def fill(a_ref):
        x = a_ref[...].astype(F32)
        if do_norm:
            x = _rms(x, g_ref[...])
        h_ref[...] = x.astype(BF16)

    @pl.when(pl.program_id(1) == 0)
    def _():
        if n_a == 1:
            fill(a_refs[0])
        else:
            @pl.when(i < np_tiles)
            def _():
                fill(a_refs[0])

            @pl.when(i >= np_tiles)
            def _():
                fill(a_refs[1])

    acc = jnp.dot(h_ref[...], w_ref[...], preferred_element_type=F32)
    if has_rope:
        half = acc.shape[1] // 2
        acc = acc[:, :half] * cos_ref[...] + acc[:, half:] * sin_ref[...]
    if scale != 1.0:
        acc = acc * scale
    if has_res:
        acc = res_ref[...] + acc
    o_ref[...] = acc.astype(o_ref.dtype)


def _mm(a, w, *, rows, tm, tn, out_dtype, k_block=0, a2=None, np_tiles=0, g=None, res=None,
        rope=None, scale=1.0, name="mm"):
    k, n = w.shape
    assert rows % tm == 0 and n % tn == 0
    n_out = n // 2 if rope is not None else n
    tn_out = tn // 2 if rope is not None else tn
    grid = (rows // tm, n // tn)

    in_specs, args = [], []
    if a2 is None:
        in_specs.append(pl.BlockSpec((tm, k), lambda i, j: (i, k_block)))
        args.append(a)
    else:
        last = np_tiles - 1
        in_specs.append(pl.BlockSpec((tm, k), lambda i, j: (jnp.minimum(i, last), k_block)))
        in_specs.append(pl.BlockSpec((tm, k), lambda i, j: (jnp.maximum(i - np_tiles, 0), k_block)))
        args += [a, a2]
    if g is not None:
        in_specs.append(pl.BlockSpec((1, k), lambda i, j: (0, 0)))
        args.append(g.reshape(1, k).astype(F32))
    in_specs.append(pl.BlockSpec((k, tn), lambda i, j: (0, j)))
    args.append(w)
    if res is not None:
        in_specs.append(pl.BlockSpec((tm, tn), lambda i, j: (i, j)))
        args.append(res)
    if rope is not None:
        for t in rope:
            in_specs.append(pl.BlockSpec((tm, tn_out), lambda i, j: (i, 0)))
            args.append(t)

    kern = functools.partial(_mm_kernel, n_a=1 if a2 is None else 2, np_tiles=np_tiles,
                             do_norm=g is not None, has_res=res is not None,
                             has_rope=rope is not None, scale=scale)
    return pl.pallas_call(
        kern,
        grid=grid,
        in_specs=in_specs,
        out_specs=pl.BlockSpec((tm, tn_out), lambda i, j: (i, j)),
        out_shape=jax.ShapeDtypeStruct((rows, n_out), out_dtype),
        scratch_shapes=[pltpu.VMEM((tm, k), BF16)],
        compiler_params=_params(("parallel", "arbitrary")),
        name=name,
    )(*args)


def _mlp_kernel(x_ref, g_ref, wu_ref, wd_ref, o_ref, h_ref):
    @pl.when(pl.program_id(1) == 0)
    def _():
        x = x_ref[...]
        h_ref[...] = _rms(x, g_ref[...]).astype(BF16)
        o_ref[...] = x

    u = jnp.dot(h_ref[...], wu_ref[...], preferred_element_type=F32)
    a = jnp.square(jnp.maximum(u, 0.0)).astype(BF16)
    o_ref[...] += jnp.dot(a, wd_ref[...], preferred_element_type=F32)


def _mlp(x, g, w_up, w_down, *, layer, tm, tf):
    t, d = x.shape
    f = w_up.shape[2]
    return pl.pallas_call(
        _mlp_kernel,
        grid=(t // tm, f // tf),
        in_specs=[
            pl.BlockSpec((tm, d), lambda i, j: (i, 0), pipeline_mode=pl.Buffered(1)),
            pl.BlockSpec((1, d), lambda i, j: (0, 0)),
            pl.BlockSpec((None, d, tf), lambda i, j: (layer, 0, j)),
            pl.BlockSpec((None, tf, d), lambda i, j: (layer, j, 0)),
        ],
        out_specs=pl.BlockSpec((tm, d), lambda i, j: (i, 0)),
        out_shape=jax.ShapeDtypeStruct((t, d), F32),
        scratch_shapes=[pltpu.VMEM((tm, d), BF16)],
        compiler_params=_params(("parallel", "arbitrary")),
        name="mlp",
    )(x, g.reshape(1, d).astype(F32), w_up, w_down)


def _norm_kernel(x_ref, g_ref, o_ref):
    o_ref[...] = _rms(x_ref[...], g_ref[...])


def _final_norm(x, g, *, tm):
    t, d = x.shape
    return pl.pallas_call(
        _norm_kernel,
        grid=(t // tm,),
        in_specs=[pl.BlockSpec((tm, d), lambda i: (i, 0)), pl.BlockSpec((1, d), lambda i: (0, 0))],
        out_specs=pl.BlockSpec((tm, d), lambda i: (i, 0)),
        out_shape=jax.ShapeDtypeStruct((t, d), F32),
        compiler_params=_params(("parallel",)),
        name="final_norm",
    )(x, g.reshape(1, d).astype(F32))


def _kv_post_kernel(p_ref, g_ref, cos_ref, sin_ref, ckv_ref, kpe_ref, *, c):
    ckv_ref[...] = _rms(p_ref[:, :c], g_ref[...])
    w = kpe_ref.shape[1]
    kpe_ref[...] = p_ref[:, c:c + w] * cos_ref[...] + p_ref[:, c + w:c + 2 * w] * sin_ref[...]


def _kv_post(proj, g_kv, cos, sin, *, c, tm):
    t = proj.shape[0]
    wb = c + 2 * V7X_LANES
    return pl.pallas_call(
        functools.partial(_kv_post_kernel, c=c),
        grid=(t // tm,),
        in_specs=[
            pl.BlockSpec((tm, wb), lambda i: (i, 0)),
            pl.BlockSpec((1, c), lambda i: (0, 0)),
            pl.BlockSpec((tm, V7X_LANES), lambda i: (i, 0)),
            pl.BlockSpec((tm, V7X_LANES), lambda i: (i, 0)),
        ],
        out_specs=[pl.BlockSpec((tm, c), lambda i: (i, 0)), pl.BlockSpec((tm, V7X_LANES), lambda i: (i, 0))],
        out_shape=[jax.ShapeDtypeStruct((t, c), F32), jax.ShapeDtypeStruct((t, V7X_LANES), F32)],
        compiler_params=_params(("parallel",)),
        name="mla_kv_post",
    )(proj, g_kv.reshape(1, c).astype(F32), cos, sin)


def _flash_step(qc, kc, vx, carry, mask=None, bias=None):
    m, acc = carry
    s = lax.dot_general(qc, kc, NT_DIMS, preferred_element_type=F32)
    if bias is not None:
        s = s + bias
    if mask is not None:
        s = jnp.where(mask, s, NEG_BIG)
    m_new = jnp.maximum(m, jnp.max(s, axis=-1, keepdims=True))
    p = jnp.exp2(s - m_new)
    acc = jnp.exp2(m - m_new) * acc + jnp.dot(p.astype(BF16), vx, preferred_element_type=F32)
    return m_new, acc


def _flash_init(rows, dv):
    return (jnp.full((rows, 1), NEG_BIG, F32), jnp.zeros((rows, dv + V7X_LANES), F32))


def _flash_out(carry, dv):
    _, acc = carry
    assert dv == V7X_LANES
    return acc[:, :dv] / acc[:, dv:]


def _mla_pattn_kernel(qn_ref, qp_ref, k_ref, v_ref, pe_ref, o_ref, kcat_ref, vx_ref, *, tq, dn, dv, nh):
    i = pl.program_id(2)
    s = k_ref.shape[0]
    half = V7X_LANES // 2

    @pl.when(i == 0)
    def _():
        pe = pe_ref[...].astype(BF16)
        ones = jnp.ones((s, V7X_LANES), BF16)
        for hh in range(nh):
            kcat_ref[hh, :, 0:dn] = k_ref[:, hh * dn:(hh + 1) * dn]
            kcat_ref[hh, :, dn:dn + V7X_LANES] = pe
            vx_ref[hh, :, 0:dv] = v_ref[:, hh * dv:(hh + 1) * dv]
            vx_ref[hh, :, dv:dv + V7X_LANES] = ones

    lane = lax.broadcasted_iota(jnp.int32, (tq, V7X_LANES), 1)
    row = lax.broadcasted_iota(jnp.int32, (tq, tq), 0)
    col = lax.broadcasted_iota(jnp.int32, (tq, tq), 1)
    causal = col <= row
    qcs = []
    for hh in range(nh):
        qp = qp_ref[:, (hh // 2) * V7X_LANES:(hh // 2 + 1) * V7X_LANES]
        keep = (lane >= half) if hh % 2 == 1 else (lane < half)
        qcs.append(jnp.concatenate([qn_ref[:, hh * dn:(hh + 1) * dn], jnp.where(keep, qp, jnp.zeros_like(qp))], axis=1))

    def blocks(off, carries, mask):
        return tuple(_flash_step(qcs[hh], kcat_ref[hh, pl.ds(off, tq), :], vx_ref[hh, pl.ds(off, tq), :],
                                 carries[hh], mask=mask) for hh in range(nh))

    init = _flash_init(tq, dv)
    carries = lax.fori_loop(0, i, lambda j, cs: blocks(pl.multiple_of(j * tq, tq), cs, None), (init,) * nh)
    carries = blocks(pl.multiple_of(i * tq, tq), carries, causal)
    for hh in range(nh):
        o_ref[:, hh * dv:(hh + 1) * dv] = _flash_out(carries[hh], dv).astype(o_ref.dtype)


def _mla_prompt_attn(q_nope, q_pe, knv, kpe_dup, *, b, s, h, dn, dv, tq):
    nq = s // tq
    nh = ATTN_HEADS_PER_STEP
    assert h % nh == 0 and nh % 2 == 0 and dn == dv
    dr = q_pe.shape[1] // h
    v_off = (h * dn) // (nh * dv)
    return pl.pallas_call(
        functools.partial(_mla_pattn_kernel, tq=tq, dn=dn, dv=dv, nh=nh),
        grid=(b, h // nh, nq),
        in_specs=[
            pl.BlockSpec((tq, nh * dn), lambda bb, p, i: (bb * nq + i, p)),
            pl.BlockSpec((tq, nh * dr), lambda bb, p, i: (bb * nq + i, p)),
            pl.BlockSpec((s, nh * dn), lambda bb, p, i: (bb, p)),
            pl.BlockSpec((s, nh * dv), lambda bb, p, i: (bb, v_off + p)),
            pl.BlockSpec((s, V7X_LANES), lambda bb, p, i: (bb, 0)),
        ],
        out_specs=pl.BlockSpec((tq, nh * dv), lambda bb, p, i: (bb * nq + i, p)),
        out_shape=jax.ShapeDtypeStruct((b * s, h * dv), BF16),
        scratch_shapes=[pltpu.VMEM((nh, s, dn + V7X_LANES), BF16), pltpu.VMEM((nh, s, dv + V7X_LANES), BF16)],
        compiler_params=_params(("parallel", "parallel", "arbitrary")),
        name="mla_prompt_attn",
    )(q_nope, q_pe, knv, knv, kpe_dup)


def _head_nt_kernel(a_ref, w_ref, o_ref):
    o_ref[0] = lax.dot_general(a_ref[...], w_ref[...], NT_DIMS, preferred_element_type=F32)


def _head_absorb(q_nope, w_uk_flat, *, h, dn, ts, row_block):
    c = w_uk_flat.shape[0]
    return pl.pallas_call(
        _head_nt_kernel,
        grid=(h,),
        in_specs=[pl.BlockSpec((ts, dn), lambda hh: (row_block, hh)), pl.BlockSpec((c, dn), lambda hh: (0, hh))],
        out_specs=pl.BlockSpec((1, ts, c), lambda hh: (hh, 0, 0)),
        out_shape=jax.ShapeDtypeStruct((h, ts, c), F32),
        compiler_params=_params(("parallel",)),
        name="mla_absorb_q",
    )(q_nope, w_uk_flat)


def _head_nn_kernel(a_ref, w_ref, o_ref):
    o_ref[...] = jnp.dot(a_ref[0].astype(BF16), w_ref[...], preferred_element_type=F32).astype(o_ref.dtype)


def _head_value(o_lat, w_uv_flat, *, h, dv):
    _, ts, c = o_lat.shape
    return pl.pallas_call(
        _head_nn_kernel,
        grid=(h,),
        in_specs=[pl.BlockSpec((1, ts, c), lambda hh: (hh, 0, 0)), pl.BlockSpec((c, dv), lambda hh: (0, hh))],
        out_specs=pl.BlockSpec((ts, dv), lambda hh: (0, hh)),
        out_shape=jax.ShapeDtypeStruct((ts, h * dv), BF16),
        compiler_params=_params(("parallel",)),
        name="mla_value_up",
    )(o_lat, w_uv_flat)


def _page_copy(stream, pt_ref, base, p, slot):
    hbm, layer, buf, sem, extent, axis = stream
    pg = pt_ref[base + p]
    span = pl.ds(pl.multiple_of(p * extent, extent), extent)
    dst = buf.at[slot, span, :] if axis == 0 else buf.at[slot, :, span]
    return pltpu.make_async_copy(hbm.at[layer, pg], dst, sem.at[slot])


def _pages_start(streams, pt_ref, bb, slot, n_pages):
    def body(p, c):
        for st in streams:
            _page_copy(st, pt_ref, bb * n_pages, p, slot).start()
        return c
    lax.fori_loop(0, n_pages, body, 0, unroll=math.gcd(n_pages, PAGE_ISSUE_UNROLL))


def _pages_wait(streams, slot):
    for _, _, buf, sem, _, _ in streams:
        pltpu.make_async_copy(buf.at[slot], buf.at[slot], sem.at[slot]).wait()


def _paged_prologue(streams, pt_ref, n_pages):
    b = pl.program_id(0)
    slot = lax.rem(b, 2)

    @pl.when(b == 0)
    def _():
        _pages_start(streams, pt_ref, b, slot, n_pages)

    @pl.when(b + 1 < pl.num_programs(0))
    def _():
        _pages_start(streams, pt_ref, b + 1, 1 - slot, n_pages)

    _pages_wait(streams, slot)
    return slot


def _pad_rows(x, rows):
    return jnp.concatenate([x, jnp.zeros((rows - x.shape[0], x.shape[1]), x.dtype)], axis=0)


def _mla_dec_kernel(pt_ref, ql_ref, qp_ref, cn_ref, pn_ref, ckv_hbm, kpe_hbm, o_ref, kbuf, pbuf, ksem, psem,
                    *, layer, n_pages, page, tk, dr):
    h, t, c = ql_ref.shape
    rows = h * t
    streams = ((ckv_hbm, layer, kbuf, ksem, page, 0), (kpe_hbm, layer, pbuf, psem, page, 1))
    slot = _paged_prologue(streams, pt_ref, n_pages)

    ql = ql_ref[...].reshape(rows, c).astype(BF16)
    qp = qp_ref[...].reshape(rows, dr).astype(BF16)

    def step(kc, s_pe, carry, mask=None):
        m, l, acc = carry
        s = lax.dot_general(ql, kc, NT_DIMS, preferred_element_type=F32) + s_pe
        if mask is not None:
            s = jnp.where(mask, s, NEG_BIG)
        m_new = jnp.maximum(m, jnp.max(s, axis=-1, keepdims=True))
        alpha = jnp.exp2(m - m_new)
        p = jnp.exp2(s - m_new)
        l = alpha * l + jnp.sum(p, axis=-1, keepdims=True)
        acc = alpha * acc + jnp.dot(p.astype(BF16), kc, preferred_element_type=F32)
        return m_new, l, acc

    def body(j, carry):
        off = pl.multiple_of(j * tk, tk)
        kc = kbuf[slot, pl.ds(off, tk), :].astype(BF16)
        pct = pbuf[slot, :, pl.ds(off, tk)].astype(BF16)
        return step(kc, jnp.dot(qp, pct, preferred_element_type=F32), carry)

    init = (jnp.full((rows, 1), NEG_BIG, F32), jnp.zeros((rows, 1), F32), jnp.zeros((rows, c), F32))
    carry = lax.fori_loop(0, (n_pages * page) // tk, body, init)

    kn = _pad_rows(cn_ref[...], V7X_LANES).astype(BF16)
    pn = _pad_rows(pn_ref[:, :dr], V7X_LANES).astype(BF16)
    qpos = lax.rem(lax.broadcasted_iota(jnp.int32, (rows, V7X_LANES), 0), t)
    kpos = lax.broadcasted_iota(jnp.int32, (rows, V7X_LANES), 1)
    s_pe = lax.dot_general(qp, pn, NT_DIMS, preferred_element_type=F32)
    m, l, acc = step(kn, s_pe, carry, mask=kpos <= qpos)
    o_ref[...] = (acc / l).reshape(h, t, c)


def _mla_decode(page_table, q_lat, q_pe_s, ckv, kpe_dup, cache_ckv, cache_kpe, *, layer, tp, tk):
    db, n_pages = page_table.shape
    h, ts, c = q_lat.shape
    t = ts // db
    dr = q_pe_s.shape[2]
    page = cache_ckv.shape[2]
    new_blk = tp // t
    grid_spec = pltpu.PrefetchScalarGridSpec(
        num_scalar_prefetch=1,
        grid=(db,),
        in_specs=[
            pl.BlockSpec((h, t, c), lambda b, pt: (0, b, 0)),
            pl.BlockSpec((h, t, dr), lambda b, pt: (0, b, 0)),
            pl.BlockSpec((t, c), lambda b, pt: (new_blk + b, 0)),
            pl.BlockSpec((t, V7X_LANES), lambda b, pt: (new_blk + b, 0)),
            pl.BlockSpec(memory_space=pl.ANY),
            pl.BlockSpec(memory_space=pl.ANY),
        ],
        out_specs=pl.BlockSpec((h, t, c), lambda b, pt: (0, b, 0)),
        scratch_shapes=[
            pltpu.VMEM((2, n_pages * page, c), F32),
            pltpu.VMEM((2, dr, n_pages * page), F32),
            pltpu.SemaphoreType.DMA((2,)),
            pltpu.SemaphoreType.DMA((2,)),
        ],
    )
    return pl.pallas_call(
        functools.partial(_mla_dec_kernel, layer=layer, n_pages=n_pages, page=page, tk=tk, dr=dr),
        grid_spec=grid_spec,
        out_shape=jax.ShapeDtypeStruct((h, ts, c), F32),
        compiler_params=_params(("arbitrary",)),
        name="mla_decode",
    )(page_table.reshape(-1), q_lat, q_pe_s, ckv, kpe_dup, cache_ckv, cache_kpe)


def _fox_kvz_kernel(x_ref, g_ref, w_ref, wt_ref, b_ref, bt_ref, o_ref, ot_ref, *, kvw):
    hb = _rms(x_ref[...], g_ref[...]).astype(BF16)
    acc = jnp.dot(hb, w_ref[...], preferred_element_type=F32)
    o_ref[:, :kvw] = acc[:, :kvw]
    o_ref[:, kvw:] = _log_sigmoid(acc[:, kvw:] + b_ref[...])
    zt = lax.dot_general(wt_ref[...], hb, NT_DIMS, preferred_element_type=F32)
    ot_ref[...] = _log_sigmoid(zt + bt_ref[...])


def _fox_kvz(x, g, w_kvz, w_ft, b_row, b_col, *, kvw, tm):
    t, d = x.shape
    n = w_kvz.shape[1]
    hh = w_ft.shape[0]
    return pl.pallas_call(
        functools.partial(_fox_kvz_kernel, kvw=kvw),
        grid=(t // tm,),
        in_specs=[
            pl.BlockSpec((tm, d), lambda i: (i, 0)),
            pl.BlockSpec((1, d), lambda i: (0, 0)),
            pl.BlockSpec((d, n), lambda i: (0, 0)),
            pl.BlockSpec((hh, d), lambda i: (0, 0)),
            pl.BlockSpec((1, n - kvw), lambda i: (0, 0)),
            pl.BlockSpec((hh, 1), lambda i: (0, 0)),
        ],
        out_specs=[pl.BlockSpec((tm, n), lambda i: (i, 0)), pl.BlockSpec((hh, tm), lambda i: (0, i))],
        out_shape=[jax.ShapeDtypeStruct((t, n), F32), jax.ShapeDtypeStruct((hh, t), F32)],
        compiler_params=_params(("parallel",)),
        name="fox_kvz",
    )(x, g.reshape(1, d).astype(F32), w_kvz, w_ft, b_row, b_col)


def _tri(n, kind, seg=0):
    j = lax.broadcasted_iota(jnp.int32, (n, n), 0)
    t = lax.broadcasted_iota(jnp.int32, (n, n), 1)
    if kind == "incl":
        m = j <= t
    elif kind == "suffix":
        m = j > t
    if seg:
        m = jnp.logical_and(m, _div_const(j, seg) == _div_const(t, seg))
    return jnp.where(m, 1.0, 0.0).astype(BF16)


def _cum_kernel(x_ref, o_ref, *, blk, seg):
    hh, n = x_ref.shape
    tri = _tri(blk, "incl", seg)
    carry = jnp.zeros((hh, 1), F32)
    for k in range(n // blk):
        x = x_ref[:, k * blk:(k + 1) * blk]
        o_ref[:, k * blk:(k + 1) * blk] = _dot3(x, tri) + carry
        if not seg:
            carry = carry + jnp.sum(x, axis=-1, keepdims=True)


def _fox_cum(lft, *, width, n_blocks, block0, seg, blk):
    hh = lft.shape[0]
    return pl.pallas_call(
        functools.partial(_cum_kernel, blk=blk, seg=seg),
        grid=(n_blocks,),
        in_specs=[pl.BlockSpec((hh, width), lambda i: (0, block0 + i))],
        out_specs=pl.BlockSpec((hh, width), lambda i: (0, i)),
        out_shape=jax.ShapeDtypeStruct((hh, width * n_blocks), F32),
        compiler_params=_params(("parallel",)),
        name="fox_cumsum",
    )(lft)


def _fox_pattn_kernel(q_ref, k_ref, v_ref, c_ref, o_ref, kx_ref, vx_ref, *, tq, dh, group):
    i = pl.program_id(2)
    s = k_ref.shape[0]
    assert 3 * group <= V7X_LANES and group % ATTN_HEADS_PER_STEP == 0

    @pl.when(i == 0)
    def _():
        c = c_ref[...] * (-LOG2E)
        ct = jnp.concatenate([c, jnp.zeros((V7X_LANES - group, s), F32)], axis=0).T
        hi = ct.astype(BF16).astype(F32)
        r1 = ct - hi
        mid = r1.astype(BF16).astype(F32)
        lo = r1 - mid
        ext = hi + pltpu.roll(mid, group, 1) + pltpu.roll(lo, 2 * group, 1)
        kx_ref[:, :dh] = k_ref[...].astype(BF16)
        kx_ref[:, dh:] = ext.astype(BF16)
        vx_ref[:, :dh] = v_ref[...].astype(BF16)
        vx_ref[:, dh:] = jnp.ones((s, V7X_LANES), BF16)

    row = lax.broadcasted_iota(jnp.int32, (tq, tq), 0)
    col = lax.broadcasted_iota(jnp.int32, (tq, tq), 1)
    causal = col <= row
    lane = lax.broadcasted_iota(jnp.int32, (tq, V7X_LANES), 1)
    lane_head = jnp.where(lane < 3 * group, lax.rem(lane, group), -1)

    nh = ATTN_HEADS_PER_STEP

    def heads(gp, _):
        los, qcs = [], []
        for e in range(nh):
            gi = nh * gp + e
            lo_ = pl.multiple_of(gi * dh, dh)
            sel = jnp.where(lane_head == gi, 1.0, 0.0).astype(BF16)
            los.append(lo_)
            qcs.append(jnp.concatenate([q_ref[:, pl.ds(lo_, dh)], sel], axis=1))

        def blocks(off, carries, mask):
            kc = kx_ref[pl.ds(off, tq), :]
            vc = vx_ref[pl.ds(off, tq), :]
            return tuple(_flash_step(qcs[e], kc, vc, carries[e], mask=mask) for e in range(nh))

        init = _flash_init(tq, dh)
        carries = lax.fori_loop(0, i, lambda j, cs: blocks(pl.multiple_of(j * tq, tq), cs, None), (init,) * nh)
        carries = blocks(pl.multiple_of(i * tq, tq), carries, causal)
        for e in range(nh):
            o_ref[:, pl.ds(los[e], dh)] = _flash_out(carries[e], dh).astype(o_ref.dtype)
        return 0

    lax.fori_loop(0, group // nh, heads, 0)


def _fox_prompt_attn(q, kvz, ct, *, b, s, nkv, group, dh, tq):
    nq = s // tq
    return pl.pallas_call(
        functools.partial(_fox_pattn_kernel, tq=tq, dh=dh, group=group),
        grid=(b, nkv, nq),
        in_specs=[
            pl.BlockSpec((tq, group * dh), lambda bb, n, i: (bb * nq + i, n)),
            pl.BlockSpec((s, dh), lambda bb, n, i: (bb, n)),
            pl.BlockSpec((s, dh), lambda bb, n, i: (bb, nkv + n)),
            pl.BlockSpec((group, s), lambda bb, n, i: (n, bb)),
        ],
        out_specs=pl.BlockSpec((tq, group * dh), lambda bb, n, i: (bb * nq + i, n)),
        out_shape=jax.ShapeDtypeStruct((b * s, nkv * group * dh), BF16),
        scratch_shapes=[pltpu.VMEM((s, dh + V7X_LANES), BF16), pltpu.VMEM((s, dh + V7X_LANES), BF16)],
        compiler_params=_params(("parallel", "parallel", "arbitrary")),
        name="fox_prompt_attn",
    )(q, kvz, kvz, ct)


def _fox_rpast_kernel(pt_ref, lf_hbm, o_ref, lbuf, sem, *, layer, n_pages, page, blk):
    hh = o_ref.shape[1]
    streams = ((lf_hbm, layer, lbuf, sem, page, 1),)
    slot = _paged_prologue(streams, pt_ref, n_pages)

    n = n_pages * page
    nblk = n // blk
    lt = jnp.concatenate([lbuf[slot, :, k * blk:(k + 1) * blk] for k in range(nblk)], axis=0)
    r = _dot3(lt, _tri(blk, "suffix"))
    tot = jnp.sum(lt, axis=-1, keepdims=True)
    carry = jnp.zeros((hh, 1), F32)
    for k in range(nblk - 1, -1, -1):
        o_ref[0, :, k * blk:(k + 1) * blk] = r[k * hh:(k + 1) * hh, :] + carry
        carry = carry + tot[k * hh:(k + 1) * hh, :]


def _fox_rpast(page_table, cache_logf, *, layer, blk):
    db, n_pages = page_table.shape
    hh, page = cache_logf.shape[2], cache_logf.shape[3]
    n = n_pages * page
    grid_spec = pltpu.PrefetchScalarGridSpec(
        num_scalar_prefetch=1,
        grid=(db,),
        in_specs=[pl.BlockSpec(memory_space=pl.ANY)],
        out_specs=pl.BlockSpec((1, hh, n), lambda b, pt: (b, 0, 0)),
        scratch_shapes=[pltpu.VMEM((2, hh, n), F32), pltpu.SemaphoreType.DMA((2,))],
    )
    return pl.pallas_call(
        functools.partial(_fox_rpast_kernel, layer=layer, n_pages=n_pages, page=page, blk=blk),
        grid_spec=grid_spec,
        out_shape=jax.ShapeDtypeStruct((db, hh, n), F32),
        compiler_params=_params(("arbitrary",)),
        name="fox_rpast",
    )(page_table.reshape(-1), cache_logf)


def _div_const(x, d):
    if d & (d - 1) == 0:
        return lax.shift_right_logical(x, d.bit_length() - 1)
    return x // d


def _fox_dec_kernel(pt_ref, q_ref, kvn_ref, r_ref, cn_ref, k_hbm, v_hbm, o_ref, kbuf, vbuf, ksem, vsem,
                    *, layer, n_pages, page, tk, nkv, group, dh):
    rows = q_ref.shape[2]
    t = rows // group
    streams = ((k_hbm, layer, kbuf, ksem, page * nkv, 0), (v_hbm, layer, vbuf, vsem, page * nkv, 0))
    slot = _paged_prologue(streams, pt_ref, n_pages)

    qpos = _div_const(lax.broadcasted_iota(jnp.int32, (rows, V7X_LANES), 0), group)
    kpos = lax.broadcasted_iota(jnp.int32, (rows, V7X_LANES), 1)
    new_mask = kpos <= qpos
    qcs = [q_ref[0, n].astype(BF16) for n in range(nkv)]

    def with_ones(v):
        return jnp.concatenate([v, jnp.ones((v.shape[0], V7X_LANES), BF16)], axis=1)

    def tiled(b16):
        return jnp.concatenate([b16] * t, axis=0)

    def body(j, carries):
        off = pl.multiple_of(j * tk, tk)
        out = []
        for n in range(nkv):
            kc = kbuf[slot, pl.ds(off * nkv + n, tk, stride=nkv), :].astype(BF16)
            vc = vbuf[slot, pl.ds(off * nkv + n, tk, stride=nkv), :].astype(BF16)
            bias = tiled(r_ref[0, n * group:(n + 1) * group, pl.ds(off, tk)] * LOG2E)
            out.append(_flash_step(qcs[n], kc, with_ones(vc), carries[n], bias=bias))
        return tuple(out)

    init = _flash_init(rows, dh)
    carries = lax.fori_loop(0, (n_pages * page) // tk, body, (init,) * nkv)

    for n in range(nkv):
        kn = _pad_rows(kvn_ref[:, n * dh:(n + 1) * dh], V7X_LANES).astype(BF16)
        vn = _pad_rows(kvn_ref[:, (nkv + n) * dh:(nkv + n + 1) * dh], V7X_LANES).astype(BF16)
        bias = tiled(cn_ref[0, n * group:(n + 1) * group, :] * (-LOG2E))
        carry = _flash_step(qcs[n], kn, with_ones(vn), carries[n], mask=new_mask, bias=bias)
        o_ref[0, n] = _flash_out(carry, dh)


def _fox_decode(page_table, q_s, kvz, r_t, cn_b, cache_k, cache_v, *, layer, tp, tk, nkv, group, dh):
    db, n_pages = page_table.shape
    rows = q_s.shape[2]
    t = rows // group
    page = cache_k.shape[2] // nkv
    n = n_pages * page
    new_blk = tp // t
    hh = nkv * group
    grid_spec = pltpu.PrefetchScalarGridSpec(
        num_scalar_prefetch=1,
        grid=(db,),
        in_specs=[
            pl.BlockSpec((1, nkv, rows, dh), lambda b, pt: (b, 0, 0, 0)),
            pl.BlockSpec((t, kvz.shape[1]), lambda b, pt: (new_blk + b, 0)),
            pl.BlockSpec((1, hh, n), lambda b, pt: (b, 0, 0)),
            pl.BlockSpec((1, hh, V7X_LANES), lambda b, pt: (b, 0, 0)),
            pl.BlockSpec(memory_space=pl.ANY),
            pl.BlockSpec(memory_space=pl.ANY),
        ],
        out_specs=pl.BlockSpec((1, nkv, rows, dh), lambda b, pt: (b, 0, 0, 0)),
        scratch_shapes=[
            pltpu.VMEM((2, n * nkv, dh), F32),
            pltpu.VMEM((2, n * nkv, dh), F32),
            pltpu.SemaphoreType.DMA((2,)),
            pltpu.SemaphoreType.DMA((2,)),
        ],
    )
    return pl.pallas_call(
        functools.partial(_fox_dec_kernel, layer=layer, n_pages=n_pages, page=page, tk=tk, nkv=nkv, group=group, dh=dh),
        grid_spec=grid_spec,
        out_shape=jax.ShapeDtypeStruct(q_s.shape, F32),
        compiler_params=_params(("arbitrary",)),
        name="fox_decode",
    )(page_table.reshape(-1), q_s, kvz, r_t, cn_b, cache_k, cache_v)


def _rot_cols(w):
    half = w.shape[-1] // 2
    return jnp.concatenate([-w[..., half:], w[..., :half]], axis=-1)


def _mla_weights(w_dq, w_uq, w_dkv, w_uk, w_uv, *, c, dn, dr, pe_tile):
    d, ql = w_dq.shape
    h = w_uq.shape[1]
    w_c, w_pe = w_dkv[:, :c], w_dkv[:, c:]
    w_rot = _rot_cols(w_pe)
    pad = jnp.zeros((d, ql - (c + 4 * dr) % ql), w_dq.dtype) if (c + 4 * dr) % ql else jnp.zeros((d, 0), w_dq.dtype)
    w1 = jnp.concatenate([w_c, w_pe, w_pe, w_rot, w_rot, pad, w_dq], axis=1).astype(BF16)
    wq_nope = w_uq[:, :, :dn].reshape(ql, h * dn).astype(BF16)
    pe = w_uq[:, :, dn:]
    hpt = pe_tile // dr
    pe_t = pe.reshape(ql, h // hpt, hpt * dr)
    rot_t = _rot_cols(pe).reshape(ql, h // hpt, hpt * dr)
    wq_pe = jnp.concatenate([pe_t, rot_t], axis=2).reshape(ql, 2 * h * dr).astype(BF16)
    w_uk_flat = w_uk.reshape(c, h * dn).astype(BF16)
    w_uv_flat = w_uv.reshape(c, -1).astype(BF16)
    w_kv_up = jnp.concatenate([w_uk_flat, w_uv_flat], axis=1)
    return w1, wq_nope, wq_pe, w_uk_flat, w_uv_flat, w_kv_up


def kernel(x_prompt, x_sample, cache_mla_ckv, cache_mla_kpe, cache_fox_k, cache_fox_v, cache_fox_logf, page_table, norm_mixer, norm_mlp, norm_final, mla_w_dq, mla_g_q, mla_w_uq, mla_w_dkv, mla_g_kv, mla_w_uk, mla_w_uv, mla_w_o, fox_w_q, fox_w_k, fox_w_v, fox_w_f, fox_b_f, fox_w_o, mlp_w_up, mlp_w_down):
    b, s, d = x_prompt.shape
    db, ds, _ = x_sample.shape
    depth = norm_mixer.shape[0]
    n_pages = page_table.shape[1]
    page = cache_mla_ckv.shape[2]
    past_len = n_pages * page
    tp, ts = b * s, db * ds
    t = tp + ts

    ql = mla_w_dq.shape[2]
    c = mla_w_uk.shape[1]
    h = mla_w_uk.shape[2]
    dn = mla_w_uk.shape[3]
    dv = mla_w_uv.shape[3]
    dr = mla_w_uq.shape[3] - dn
    mla_scale = float(dn + dr) ** -0.5

    fh = fox_w_f.shape[2]
    nkv, dh = cache_fox_k.shape[3], cache_fox_k.shape[4]
    group = fh // nkv
    fox_scale = float(dh) ** -0.5

    tm = _row_tile(tp, ts, ROW_TILE)
    tn = N_TILE
    tq = _largest_tile(s, Q_TILE, V7X_LANES)
    tk_mla = _largest_tile(past_len, MLA_KV_CHUNK, page)
    tk_fox = _largest_tile(past_len, FOX_KV_CHUNK, page)
    tm_mlp = _largest_tile(t, MLP_ROW_TILE, V7X_BF16_SUBLANES)
    tf = _largest_tile(mlp_w_up.shape[2], MLP_F_TILE, V7X_LANES)
    pe_tile = 8 * dr
    np_tiles = tp // tm

    def col_tile(n, k):
        return _largest_tile(n, min(N_TILE_MAX, max(N_TILE, N_TILE_BYTES // (2 * k))), V7X_LANES)

    mlp_up_b = mlp_w_up.astype(BF16)
    mlp_down_b = mlp_w_down.astype(BF16)

    x = jnp.concatenate([x_prompt.reshape(tp, d), x_sample.reshape(ts, d)], axis=0)

    half = dr // 2
    inv_freq = ROPE_THETA ** (-jnp.arange(half, dtype=F32) / half)
    pos = jnp.concatenate([jnp.tile(jnp.arange(s), b), jnp.tile(past_len + jnp.arange(ds), db)])
    ang = pos.astype(F32)[:, None] * inv_freq[None, :]
    cos_t = jnp.tile(jnp.cos(ang), (1, pe_tile // half))
    sin_t = jnp.tile(jnp.sin(ang), (1, pe_tile // half))

    cache_k2 = cache_fox_k.reshape(cache_fox_k.shape[:2] + (page * nkv, dh))
    cache_v2 = cache_fox_v.reshape(cache_fox_v.shape[:2] + (page * nkv, dh))
    cache_kpe_t = jnp.swapaxes(cache_mla_kpe, 2, 3)
    cache_logf_t = jnp.swapaxes(cache_fox_logf, 2, 3)

    outs = {k: [] for k in ("ckv", "kpe", "fk", "fv", "fl")}

    for i in range(depth):
        li = i // 2
        if i % 2 == 0:
            w1, wq_nope, wq_pe, w_uk_flat, w_uv_flat, w_kv_up = _mla_weights(
                mla_w_dq[li], mla_w_uq[li], mla_w_dkv[li], mla_w_uk[li], mla_w_uv[li], c=c, dn=dn, dr=dr, pe_tile=pe_tile)
            w_o = mla_w_o[li].astype(BF16)
            kvw = w1.shape[1] - ql
            proj = _mm(x, w1, rows=t, tm=tm, tn=col_tile(w1.shape[1], d), out_dtype=F32, g=norm_mixer[i],
                       name="mla_down")
            ckv, kpe_dup = _kv_post(proj, mla_g_kv[li], cos_t, sin_t, c=c, tm=tm)
            outs["ckv"].append(ckv)
            outs["kpe"].append(kpe_dup[:, :dr])
            q_nope = _mm(proj, wq_nope, rows=t, tm=tm, tn=col_tile(h * dn, ql), out_dtype=BF16, k_block=kvw // ql,
                         g=mla_g_q[li], scale=mla_scale * LOG2E, name="mla_q_nope")
            q_pe = _mm(proj, wq_pe, rows=t, tm=tm, tn=2 * pe_tile, out_dtype=BF16, k_block=kvw // ql, g=mla_g_q[li],
                       rope=(cos_t, sin_t), scale=mla_scale * LOG2E, name="mla_q_pe")
            knv = _mm(ckv, w_kv_up, rows=tp, tm=tm, tn=col_tile(w_kv_up.shape[1], c), out_dtype=BF16, name="mla_kv_up")
            attn_p = _mla_prompt_attn(q_nope, q_pe, knv, kpe_dup, b=b, s=s, h=h, dn=dn, dv=dv, tq=tq)
            q_lat = _head_absorb(q_nope, w_uk_flat, h=h, dn=dn, ts=ts, row_block=tp // ts)
            q_pe_s = q_pe[tp:].reshape(ts, h, dr).transpose(1, 0, 2).astype(F32)
            o_lat = _mla_decode(page_table, q_lat, q_pe_s, ckv, kpe_dup, cache_mla_ckv, cache_kpe_t,
                                layer=li, tp=tp, tk=tk_mla)
            attn_s = _head_value(o_lat, w_uv_flat, h=h, dv=dv)
        else:
            w_q = fox_w_q[li].astype(BF16)
            w_o = fox_w_o[li].astype(BF16)
            zpad = V7X_LANES - fh
            w_kvz = jnp.concatenate([fox_w_k[li], fox_w_v[li], fox_w_f[li], jnp.zeros((d, zpad), F32)], axis=1).astype(BF16)
            w_ft = fox_w_f[li].T.astype(BF16)
            b_row = jnp.concatenate([fox_b_f[li], jnp.zeros((zpad,), F32)]).reshape(1, V7X_LANES)
            b_col = fox_b_f[li].reshape(fh, 1)
            kvw = 2 * nkv * dh
            q = _mm(x, w_q, rows=t, tm=tm, tn=col_tile(w_q.shape[1], d), out_dtype=BF16, g=norm_mixer[i],
                    scale=fox_scale * LOG2E, name="fox_q")
            kvz, lft = _fox_kvz(x, norm_mixer[i], w_kvz, w_ft, b_row, b_col, kvw=kvw, tm=tm)
            outs["fk"].append(kvz[:, :nkv * dh])
            outs["fv"].append(kvz[:, nkv * dh:kvw])
            outs["fl"].append(kvz[:, kvw:kvw + fh])
            blk = 2 * V7X_LANES
            ct = _fox_cum(lft, width=s, n_blocks=b, block0=0, seg=0, blk=blk)
            attn_p = _fox_prompt_attn(q, kvz, ct, b=b, s=s, nkv=nkv, group=group, dh=dh, tq=tq)
            wn = _largest_tile(ts, blk, math.lcm(ds, V7X_LANES))
            cnt = _fox_cum(lft, width=wn, n_blocks=ts // wn, block0=tp // wn, seg=ds, blk=wn)
            cn_b = cnt.reshape(fh, db, ds).transpose(1, 0, 2)
            cn_b = jnp.concatenate([cn_b, jnp.zeros((db, fh, V7X_LANES - ds), F32)], axis=2)
            r_t = _fox_rpast(page_table, cache_logf_t, layer=li, blk=blk)
            q_s = q[tp:].astype(F32).reshape(db, ds, nkv, group, dh).transpose(0, 2, 1, 3, 4)
            o_s = _fox_decode(page_table, q_s.reshape(db, nkv, ds * group, dh), kvz, r_t, cn_b, cache_k2, cache_v2,
                              layer=li, tp=tp, tk=tk_fox, nkv=nkv, group=group, dh=dh)
            attn_s = o_s.reshape(db, nkv, ds, group, dh).transpose(0, 2, 1, 3, 4).reshape(ts, fh * dh)
        x = _mm(attn_p, w_o, rows=t, tm=tm, tn=tn, out_dtype=F32, a2=attn_s, np_tiles=np_tiles, res=x, name="attn_out")
        x = _mlp(x, norm_mlp[i], mlp_up_b, mlp_down_b, layer=i, tm=tm_mlp, tf=tf)

    y = _final_norm(x, norm_final, tm=tm)

    def split(rows_list, tail):
        st = jnp.stack(rows_list)
        n_l = st.shape[0]
        return st[:, :tp].reshape((n_l, b, s) + tail), st[:, tp:].reshape((n_l, db, ds) + tail)

    p_ckv, s_ckv = split(outs["ckv"], (c,))
    p_kpe, s_kpe = split(outs["kpe"], (dr,))
    p_fk, s_fk = split(outs["fk"], (nkv, dh))
    p_fv, s_fv = split(outs["fv"], (nkv, dh))
    p_fl, s_fl = split(outs["fl"], (fh,))
    return (y[:tp].reshape(b, s, d), y[tp:].reshape(db, ds, d),
            p_ckv, p_kpe, p_fk, p_fv, p_fl, s_ckv, s_kpe, s_fk, s_fv, s_fl)
```

```python
import functools
import math

import jax
import jax.numpy as jnp
from jax import lax
from jax.experimental import pallas as pl
from jax.experimental.pallas import tpu as pltpu

F32 = jnp.float32
BF16 = jnp.bfloat16

RMS_EPS = 1e-6
ROPE_THETA = 10000.0
NEG_BIG = -1e30

V7X_LANES = 128
V7X_BF16_SUBLANES = 16
V7X_VMEM_LIMIT_BYTES = 56 * 1024 * 1024

NT_DIMS = (((1,), (1,)), ((), ()))

ROW_TILE = 512
N_TILE = 512
Q_TILE = 512
ATTN_HEADS_PER_STEP = 8
MLA_KV_CHUNK = 2048
FOX_KV_CHUNK = 4096
MLP_ROW_TILE = 512
MLP_F_TILE = 512
PAGE_ISSUE_UNROLL = 8
N_TILE_BYTES = 8 * 1024 * 1024
N_TILE_MAX = 2048

LOG2E = 1.4426950408889634


def _params(semantics, vmem=V7X_VMEM_LIMIT_BYTES):
    return pltpu.CompilerParams(dimension_semantics=semantics, vmem_limit_bytes=vmem)


def _largest_tile(n, target, multiple):
    best = None
    for t in range(multiple, min(n, target) + 1, multiple):
        if n % t == 0:
            best = t
    assert best is not None, (n, target, multiple)
    return best


def _row_tile(tp, ts, target):
    return _largest_tile(math.gcd(tp, ts), target, V7X_BF16_SUBLANES)


def _rms(x, g, eps=RMS_EPS):
    r = lax.rsqrt(jnp.mean(x * x, axis=-1, keepdims=True) + eps)
    return (x * r) * g


def _log_sigmoid(x):
    return jnp.minimum(x, 0.0) - jnp.log1p(jnp.exp(-jnp.abs(x)))


def _split3(x):
    hi = x.astype(BF16)
    r1 = x - hi.astype(F32)
    mid = r1.astype(BF16)
    lo = (r1 - mid.astype(F32)).astype(BF16)
    return hi, mid, lo


def _dot3(x, m):
    hi, mid, lo = _split3(x)
    out = jnp.dot(lo, m, preferred_element_type=F32)
    out = out + jnp.dot(mid, m, preferred_element_type=F32)
    return out + jnp.dot(hi, m, preferred_element_type=F32)


def _mm_kernel(*refs, n_a, np_tiles, do_norm, has_res, has_rope, scale):
    a_refs = refs[:n_a]
    pos = n_a
    g_ref = None
    if do_norm:
        g_ref = refs[pos]
        pos += 1
    w_ref = refs[pos]
    pos += 1
    res_ref = cos_ref = sin_ref = None
    if has_res:
        res_ref = refs[pos]
        pos += 1
    if has_rope:
        cos_ref, sin_ref = refs[pos], refs[pos + 1]
        pos += 2
    o_ref, h_ref = refs[pos], refs[pos + 1]

    i = pl.program_id(0)

    def fill(a_ref):
        x = a_ref[...].astype(F32)
        if do_norm:
            x = _rms(x, g_ref[...])
        h_ref[...] = x.astype(BF16)

    @pl.when(pl.program_id(1) == 0)
    def _():
        if n_a == 1:
            fill(a_refs[0])
        else:
            @pl.when(i < np_tiles)
            def _():
                fill(a_refs[0])

            @pl.when(i >= np_tiles)
            def _():
                fill(a_refs[1])

    acc = jnp.dot(h_ref[...], w_ref[...], preferred_element_type=F32)
    if has_rope:
        half = acc.shape[1] // 2
        acc = acc[:, :half] * cos_ref[...] + acc[:, half:] * sin_ref[...]
    if scale != 1.0:
        acc = acc * scale
    if has_res:
        acc = res_ref[...] + acc
    o_ref[...] = acc.astype(o_ref.dtype)


def _mm(a, w, *, rows, tm, tn, out_dtype, k_block=0, a2=None, np_tiles=0, g=None, res=None,
        rope=None, scale=1.0, name="mm"):
    k, n = w.shape
    assert rows % tm == 0 and n % tn == 0
    n_out = n // 2 if rope is not None else n
    tn_out = tn // 2 if rope is not None else tn
    grid = (rows // tm, n // tn)

    in_specs, args = [], []
    if a2 is None:
        in_specs.append(pl.BlockSpec((tm, k), lambda i, j: (i, k_block)))
        args.append(a)
    else:
        last = np_tiles - 1
        in_specs.append(pl.BlockSpec((tm, k), lambda i, j: (jnp.minimum(i, last), k_block)))
        in_specs.append(pl.BlockSpec((tm, k), lambda i, j: (jnp.maximum(i - np_tiles, 0), k_block)))
        args += [a, a2]
    if g is not None:
        in_specs.append(pl.BlockSpec((1, k), lambda i, j: (0, 0)))
        args.append(g.reshape(1, k).astype(F32))
    in_specs.append(pl.BlockSpec((k, tn), lambda i, j: (0, j)))
    args.append(w)
    if res is not None:
        in_specs.append(pl.BlockSpec((tm, tn), lambda i, j: (i, j)))
        args.append(res)
    if rope is not None:
        for t in rope:
            in_specs.append(pl.BlockSpec((tm, tn_out), lambda i, j: (i, 0)))
            args.append(t)

    kern = functools.partial(_mm_kernel, n_a=1 if a2 is None else 2, np_tiles=np_tiles,
                             do_norm=g is not None, has_res=res is not None,
                             has_rope=rope is not None, scale=scale)
    return pl.pallas_call(
        kern,
        grid=grid,
        in_specs=in_specs,
        out_specs=pl.BlockSpec((tm, tn_out), lambda i, j: (i, j)),
        out_shape=jax.ShapeDtypeStruct((rows, n_out), out_dtype),
        scratch_shapes=[pltpu.VMEM((tm, k), BF16)],
        compiler_params=_params(("parallel", "arbitrary")),
        name=name,
    )(*args)


def _mlp_kernel(x_ref, g_ref, wu_ref, wd_ref, o_ref, h_ref):
    @pl.when(pl.program_id(1) == 0)
    def _():
        x = x_ref[...]
        h_ref[...] = _rms(x, g_ref[...]).astype(BF16)
        o_ref[...] = x

    u = jnp.dot(h_ref[...], wu_ref[...], preferred_element_type=F32)
    a = jnp.square(jnp.maximum(u, 0.0)).astype(BF16)
    o_ref[...] += jnp.dot(a, wd_ref[...], preferred_element_type=F32)


def _mlp(x, g, w_up, w_down, *, layer, tm, tf):
    t, d = x.shape
    f = w_down.shape[1]
    return pl.pallas_call(
        _mlp_kernel,
        grid=(t // tm, f // tf),
        in_specs=[
            pl.BlockSpec((tm, d), lambda i, j: (i, 0), pipeline_mode=pl.Buffered(1)),
            pl.BlockSpec((1, d), lambda i, j: (0, 0)),
            pl.BlockSpec((None, None, d, tf), lambda i, j: (layer, j, 0, 0)),
            pl.BlockSpec((None, tf, d), lambda i, j: (layer, j, 0)),
        ],
        out_specs=pl.BlockSpec((tm, d), lambda i, j: (i, 0)),
        out_shape=jax.ShapeDtypeStruct((t, d), F32),
        scratch_shapes=[pltpu.VMEM((tm, d), BF16)],
        compiler_params=_params(("parallel", "arbitrary")),
        name="mlp",
    )(x, g.reshape(1, d).astype(F32), w_up, w_down)


def _tile_cast_kernel(w_ref, o_ref):
    o_ref[...] = w_ref[...].astype(o_ref.dtype)


def _tile_cast(w, tf):
    n_l, d, f = w.shape
    return pl.pallas_call(
        _tile_cast_kernel,
        grid=(n_l, f // tf),
        in_specs=[pl.BlockSpec((None, d, tf), lambda l, j: (l, 0, j))],
        out_specs=pl.BlockSpec((None, None, d, tf), lambda l, j: (l, j, 0, 0)),
        out_shape=jax.ShapeDtypeStruct((n_l, f // tf, d, tf), BF16),
        compiler_params=_params(("parallel", "parallel")),
        name="mlp_up_tiles",
    )(w)


def _norm_kernel(x_ref, g_ref, o_ref):
    o_ref[...] = _rms(x_ref[...], g_ref[...])


def _final_norm(x, g, *, row0, rows, tm):
    d = x.shape[1]
    blk0 = row0 // tm
    return pl.pallas_call(
        _norm_kernel,
        grid=(rows // tm,),
        in_specs=[pl.BlockSpec((tm, d), lambda i: (blk0 + i, 0)), pl.BlockSpec((1, d), lambda i: (0, 0))],
        out_specs=pl.BlockSpec((tm, d), lambda i: (i, 0)),
        out_shape=jax.ShapeDtypeStruct((rows, d), F32),
        compiler_params=_params(("parallel",)),
        name="final_norm",
    )(x, g.reshape(1, d).astype(F32))


def _kv_post_kernel(p_ref, g_ref, cos_ref, sin_ref, ckv_ref, kpe_ref, *, c):
    ckv_ref[...] = _rms(p_ref[:, :c], g_ref[...])
    w = kpe_ref.shape[1]
    kpe_ref[...] = p_ref[:, c:c + w] * cos_ref[...] + p_ref[:, c + w:c + 2 * w] * sin_ref[...]


def _kv_post(proj, g_kv, cos, sin, *, c, tm):
    t = proj.shape[0]
    wb = c + 2 * V7X_LANES
    return pl.pallas_call(
        functools.partial(_kv_post_kernel, c=c),
        grid=(t // tm,),
        in_specs=[
            pl.BlockSpec((tm, wb), lambda i: (i, 0)),
            pl.BlockSpec((1, c), lambda i: (0, 0)),
            pl.BlockSpec((tm, V7X_LANES), lambda i: (i, 0)),
            pl.BlockSpec((tm, V7X_LANES), lambda i: (i, 0)),
        ],
        out_specs=[pl.BlockSpec((tm, c), lambda i: (i, 0)), pl.BlockSpec((tm, V7X_LANES), lambda i: (i, 0))],
        out_shape=[jax.ShapeDtypeStruct((t, c), F32), jax.ShapeDtypeStruct((t, V7X_LANES), F32)],
        compiler_params=_params(("parallel",)),
        name="mla_kv_post",
    )(proj, g_kv.reshape(1, c).astype(F32), cos, sin)


def _flash_step(qc, kc, vx, carry, mask=None, bias=None):
    m, acc = carry
    s = lax.dot_general(qc, kc, NT_DIMS, preferred_element_type=F32)
    if bias is not None:
        s = s + bias
    if mask is not None:
        s = jnp.where(mask, s, NEG_BIG)
    m_new = jnp.maximum(m, jnp.max(s, axis=-1, keepdims=True))
    p = jnp.exp2(s - m_new)
    acc = jnp.exp2(m - m_new) * acc + jnp.dot(p.astype(BF16), vx, preferred_element_type=F32)
    return m_new, acc


def _flash_init(rows, dv):
    return (jnp.full((rows, 1), NEG_BIG, F32), jnp.zeros((rows, dv + V7X_LANES), F32))


def _flash_out(carry, dv):
    _, acc = carry
    assert dv == V7X_LANES
    return acc[:, :dv] / acc[:, dv:]


def _mla_pattn_kernel(qn_ref, qp_ref, k_ref, v_ref, pe_ref, o_ref, kcat_ref, vx_ref, *, tq, dn, dv, nh):
    i = pl.program_id(2)
    s = k_ref.shape[0]
    half = V7X_LANES // 2

    @pl.when(i == 0)
    def _():
        pe = pe_ref[...].astype(BF16)
        ones = jnp.ones((s, V7X_LANES), BF16)
        for hh in range(nh):
            kcat_ref[hh, :, 0:dn] = k_ref[:, hh * dn:(hh + 1) * dn]
            kcat_ref[hh, :, dn:dn + V7X_LANES] = pe
            vx_ref[hh, :, 0:dv] = v_ref[:, hh * dv:(hh + 1) * dv]
            vx_ref[hh, :, dv:dv + V7X_LANES] = ones

    lane = lax.broadcasted_iota(jnp.int32, (tq, V7X_LANES), 1)
    row = lax.broadcasted_iota(jnp.int32, (tq, tq), 0)
    col = lax.broadcasted_iota(jnp.int32, (tq, tq), 1)
    causal = col <= row
    qcs = []
    for hh in range(nh):
        qp = qp_ref[:, (hh // 2) * V7X_LANES:(hh // 2 + 1) * V7X_LANES]
        keep = (lane >= half) if hh % 2 == 1 else (lane < half)
        qcs.append(jnp.concatenate([qn_ref[:, hh * dn:(hh + 1) * dn], jnp.where(keep, qp, jnp.zeros_like(qp))], axis=1))

    def blocks(off, carries, mask):
        return tuple(_flash_step(qcs[hh], kcat_ref[hh, pl.ds(off, tq), :], vx_ref[hh, pl.ds(off, tq), :],
                                 carries[hh], mask=mask) for hh in range(nh))

    init = _flash_init(tq, dv)
    carries = lax.fori_loop(0, i, lambda j, cs: blocks(pl.multiple_of(j * tq, tq), cs, None), (init,) * nh)
    carries = blocks(pl.multiple_of(i * tq, tq), carries, causal)
    for hh in range(nh):
        o_ref[:, hh * dv:(hh + 1) * dv] = _flash_out(carries[hh], dv).astype(o_ref.dtype)


def _mla_prompt_attn(q_nope, q_pe, knv, kpe_dup, *, b, s, h, dn, dv, tq):
    nq = s // tq
    nh = ATTN_HEADS_PER_STEP
    assert h % nh == 0 and nh % 2 == 0 and dn == dv
    dr = q_pe.shape[1] // h
    v_off = (h * dn) // (nh * dv)
    return pl.pallas_call(
        functools.partial(_mla_pattn_kernel, tq=tq, dn=dn, dv=dv, nh=nh),
        grid=(b, h // nh, nq),
        in_specs=[
            pl.BlockSpec((tq, nh * dn), lambda bb, p, i: (bb * nq + i, p)),
            pl.BlockSpec((tq, nh * dr), lambda bb, p, i: (bb * nq + i, p)),
            pl.BlockSpec((s, nh * dn), lambda bb, p, i: (bb, p)),
            pl.BlockSpec((s, nh * dv), lambda bb, p, i: (bb, v_off + p)),
            pl.BlockSpec((s, V7X_LANES), lambda bb, p, i: (bb, 0)),
        ],
        out_specs=pl.BlockSpec((tq, nh * dv), lambda bb, p, i: (bb * nq + i, p)),
        out_shape=jax.ShapeDtypeStruct((b * s, h * dv), BF16),
        scratch_shapes=[pltpu.VMEM((nh, s, dn + V7X_LANES), BF16), pltpu.VMEM((nh, s, dv + V7X_LANES), BF16)],
        compiler_params=_params(("parallel", "parallel", "arbitrary")),
        name="mla_prompt_attn",
    )(q_nope, q_pe, knv, knv, kpe_dup)


def _head_nt_kernel(a_ref, w_ref, o_ref):
    o_ref[0] = lax.dot_general(a_ref[...], w_ref[...], NT_DIMS, preferred_element_type=F32)


def _head_absorb(q_nope, w_uk_flat, *, h, dn, ts, row_block):
    c = w_uk_flat.shape[0]
    return pl.pallas_call(
        _head_nt_kernel,
        grid=(h,),
        in_specs=[pl.BlockSpec((ts, dn), lambda hh: (row_block, hh)), pl.BlockSpec((c, dn), lambda hh: (0, hh))],
        out_specs=pl.BlockSpec((1, ts, c), lambda hh: (hh, 0, 0)),
        out_shape=jax.ShapeDtypeStruct((h, ts, c), F32),
        compiler_params=_params(("parallel",)),
        name="mla_absorb_q",
    )(q_nope, w_uk_flat)


def _head_nn_kernel(a_ref, w_ref, o_ref):
    o_ref[...] = jnp.dot(a_ref[0].astype(BF16), w_ref[...], preferred_element_type=F32).astype(o_ref.dtype)


def _head_value(o_lat, w_uv_flat, *, h, dv):
    _, ts, c = o_lat.shape
    return pl.pallas_call(
        _head_nn_kernel,
        grid=(h,),
        in_specs=[pl.BlockSpec((1, ts, c), lambda hh: (hh, 0, 0)), pl.BlockSpec((c, dv), lambda hh: (0, hh))],
        out_specs=pl.BlockSpec((ts, dv), lambda hh: (0, hh)),
        out_shape=jax.ShapeDtypeStruct((ts, h * dv), BF16),
        compiler_params=_params(("parallel",)),
        name="mla_value_up",
    )(o_lat, w_uv_flat)


def _page_copy(stream, pt_ref, base, p, slot):
    hbm, layer, buf, sem, extent, axis = stream
    pg = pt_ref[base + p]
    span = pl.ds(pl.multiple_of(p * extent, extent), extent)
    dst = buf.at[slot, span, :] if axis == 0 else buf.at[slot, :, span]
    return pltpu.make_async_copy(hbm.at[layer, pg], dst, sem.at[slot])


def _pages_start(streams, pt_ref, bb, slot, n_pages):
    def body(p, c):
        for st in streams:
            _page_copy(st, pt_ref, bb * n_pages, p, slot).start()
        return c
    lax.fori_loop(0, n_pages, body, 0, unroll=math.gcd(n_pages, PAGE_ISSUE_UNROLL))


def _pages_wait(streams, slot):
    for _, _, buf, sem, _, _ in streams:
        pltpu.make_async_copy(buf.at[slot], buf.at[slot], sem.at[slot]).wait()


def _paged_prologue(streams, pt_ref, n_pages):
    b = pl.program_id(0)
    slot = lax.rem(b, 2)

    @pl.when(b == 0)
    def _():
        _pages_start(streams, pt_ref, b, slot, n_pages)

    @pl.when(b + 1 < pl.num_programs(0))
    def _():
        _pages_start(streams, pt_ref, b + 1, 1 - slot, n_pages)

    _pages_wait(streams, slot)
    return slot


def _paged_begin(streams, pt_ref, n_pages):
    b = pl.program_id(0)
    slot = lax.rem(b, 2)

    @pl.when(b == 0)
    def _():
        _pages_start(streams, pt_ref, b, slot, n_pages)

    _pages_wait(streams, slot)
    return slot, lax.rem(b + 1, pl.num_programs(0))


def _pages_issue(streams, pt_ref, bb, slot, n_pages, p0, count):
    for q in range(count):
        for st in streams:
            _page_copy(st, pt_ref, bb * n_pages, p0 + q, slot).start()


def _paged_end(streams, slot):
    @pl.when(pl.program_id(0) == pl.num_programs(0) - 1)
    def _():
        _pages_wait(streams, 1 - slot)


def _pad_rows(x, rows):
    return jnp.concatenate([x, jnp.zeros((rows - x.shape[0], x.shape[1]), x.dtype)], axis=0)


def _mla_dec_kernel(pt_ref, ql_ref, qp_ref, cn_ref, pn_ref, ckv_hbm, kpe_hbm, o_ref, kbuf, pbuf, ksem, psem,
                    *, layer, n_pages, page, tk, dr):
    h, t, c = ql_ref.shape
    rows = h * t
    streams = ((ckv_hbm, layer, kbuf, ksem, page, 0), (kpe_hbm, layer, pbuf, psem, page, 1))
    slot, nxt = _paged_begin(streams, pt_ref, n_pages)
    n_chunks = (n_pages * page) // tk
    ppc = n_pages // n_chunks
    assert ppc * n_chunks == n_pages

    ql = ql_ref[...].reshape(rows, c).astype(BF16)
    qp = qp_ref[...].reshape(rows, dr).astype(BF16)

    def step(kc, s_pe, carry, mask=None):
        m, l, acc = carry
        s = lax.dot_general(ql, kc, NT_DIMS, preferred_element_type=F32) + s_pe
        if mask is not None:
            s = jnp.where(mask, s, NEG_BIG)
        m_new = jnp.maximum(m, jnp.max(s, axis=-1, keepdims=True))
        alpha = jnp.exp2(m - m_new)
        p = jnp.exp2(s - m_new)
        l = alpha * l + jnp.sum(p, axis=-1, keepdims=True)
        acc = alpha * acc + jnp.dot(p.astype(BF16), kc, preferred_element_type=F32)
        return m_new, l, acc

    def body(j, carry):
        _pages_issue(streams, pt_ref, nxt, 1 - slot, n_pages, j * ppc, ppc)
        off = pl.multiple_of(j * tk, tk)
        kc = kbuf[slot, pl.ds(off, tk), :].astype(BF16)
        pct = pbuf[slot, :, pl.ds(off, tk)].astype(BF16)
        return step(kc, jnp.dot(qp, pct, preferred_element_type=F32), carry)

    init = (jnp.full((rows, 1), NEG_BIG, F32), jnp.zeros((rows, 1), F32), jnp.zeros((rows, c), F32))
    carry = lax.fori_loop(0, n_chunks, body, init)
    _paged_end(streams, slot)

    kn = _pad_rows(cn_ref[...], V7X_LANES).astype(BF16)
    pn = _pad_rows(pn_ref[:, :dr], V7X_LANES).astype(BF16)
    qpos = lax.rem(lax.broadcasted_iota(jnp.int32, (rows, V7X_LANES), 0), t)
    kpos = lax.broadcasted_iota(jnp.int32, (rows, V7X_LANES), 1)
    s_pe = lax.dot_general(qp, pn, NT_DIMS, preferred_element_type=F32)
    m, l, acc = step(kn, s_pe, carry, mask=kpos <= qpos)
    o_ref[...] = (acc / l).reshape(h, t, c)


def _mla_decode(page_table, q_lat, q_pe_s, ckv, kpe_dup, cache_ckv, cache_kpe, *, layer, tp, tk):
    db, n_pages = page_table.shape
    h, ts, c = q_lat.shape
    t = ts // db
    dr = q_pe_s.shape[2]
    page = cache_ckv.shape[2]
    new_blk = tp // t
    grid_spec = pltpu.PrefetchScalarGridSpec(
        num_scalar_prefetch=1,
        grid=(db,),
        in_specs=[
            pl.BlockSpec((h, t, c), lambda b, pt: (0, b, 0)),
            pl.BlockSpec((h, t, dr), lambda b, pt: (0, b, 0)),
            pl.BlockSpec((t, c), lambda b, pt: (new_blk + b, 0)),
            pl.BlockSpec((t, V7X_LANES), lambda b, pt: (new_blk + b, 0)),
            pl.BlockSpec(memory_space=pl.ANY),
            pl.BlockSpec(memory_space=pl.ANY),
        ],
        out_specs=pl.BlockSpec((h, t, c), lambda b, pt: (0, b, 0)),
        scratch_shapes=[
            pltpu.VMEM((2, n_pages * page, c), F32),
            pltpu.VMEM((2, dr, n_pages * page), F32),
            pltpu.SemaphoreType.DMA((2,)),
            pltpu.SemaphoreType.DMA((2,)),
        ],
    )
    return pl.pallas_call(
        functools.partial(_mla_dec_kernel, layer=layer, n_pages=n_pages, page=page, tk=tk, dr=dr),
        grid_spec=grid_spec,
        out_shape=jax.ShapeDtypeStruct((h, ts, c), F32),
        compiler_params=_params(("arbitrary",)),
        name="mla_decode",
    )(page_table.reshape(-1), q_lat, q_pe_s, ckv, kpe_dup, cache_ckv, cache_kpe)


def _fox_kvz_kernel(x_ref, g_ref, w_ref, wt_ref, b_ref, bt_ref, o_ref, ot_ref, *, kvw):
    hb = _rms(x_ref[...], g_ref[...]).astype(BF16)
    acc = jnp.dot(hb, w_ref[...], preferred_element_type=F32)
    o_ref[:, :kvw] = acc[:, :kvw]
    o_ref[:, kvw:] = _log_sigmoid(acc[:, kvw:] + b_ref[...])
    zt = lax.dot_general(wt_ref[...], hb, NT_DIMS, preferred_element_type=F32)
    ot_ref[...] = _log_sigmoid(zt + bt_ref[...])


def _fox_kvz(x, g, w_kvz, w_ft, b_row, b_col, *, kvw, tm):
    t, d = x.shape
    n = w_kvz.shape[1]
    hh = w_ft.shape[0]
    return pl.pallas_call(
        functools.partial(_fox_kvz_kernel, kvw=kvw),
        grid=(t // tm,),
        in_specs=[
            pl.BlockSpec((tm, d), lambda i: (i, 0)),
            pl.BlockSpec((1, d), lambda i: (0, 0)),
            pl.BlockSpec((d, n), lambda i: (0, 0)),
            pl.BlockSpec((hh, d), lambda i: (0, 0)),
            pl.BlockSpec((1, n - kvw), lambda i: (0, 0)),
            pl.BlockSpec((hh, 1), lambda i: (0, 0)),
        ],
        out_specs=[pl.BlockSpec((tm, n), lambda i: (i, 0)), pl.BlockSpec((hh, tm), lambda i: (0, i))],
        out_shape=[jax.ShapeDtypeStruct((t, n), F32), jax.ShapeDtypeStruct((hh, t), F32)],
        compiler_params=_params(("parallel",)),
        name="fox_kvz",
    )(x, g.reshape(1, d).astype(F32), w_kvz, w_ft, b_row, b_col)


def _tri(n, kind, seg=0):
    j = lax.broadcasted_iota(jnp.int32, (n, n), 0)
    t = lax.broadcasted_iota(jnp.int32, (n, n), 1)
    if kind == "incl":
        m = j <= t
    elif kind == "suffix":
        m = j > t
    if seg:
        m = jnp.logical_and(m, _div_const(j, seg) == _div_const(t, seg))
    return jnp.where(m, 1.0, 0.0).astype(BF16)


def _cum_kernel(x_ref, o_ref, *, blk, seg):
    hh, n = x_ref.shape
    tri = _tri(blk, "incl", seg)
    carry = jnp.zeros((hh, 1), F32)
    for k in range(n // blk):
        x = x_ref[:, k * blk:(k + 1) * blk]
        o_ref[:, k * blk:(k + 1) * blk] = _dot3(x, tri) + carry
        if not seg:
            carry = carry + jnp.sum(x, axis=-1, keepdims=True)


def _fox_cum(lft, *, width, n_blocks, block0, seg, blk):
    hh = lft.shape[0]
    return pl.pallas_call(
        functools.partial(_cum_kernel, blk=blk, seg=seg),
        grid=(n_blocks,),
        in_specs=[pl.BlockSpec((hh, width), lambda i: (0, block0 + i))],
        out_specs=pl.BlockSpec((hh, width), lambda i: (0, i)),
        out_shape=jax.ShapeDtypeStruct((hh, width * n_blocks), F32),
        compiler_params=_params(("parallel",)),
        name="fox_cumsum",
    )(lft)


def _fox_pattn_kernel(q_ref, k_ref, v_ref, c_ref, o_ref, kx_ref, vx_ref, *, tq, dh, group):
    i = pl.program_id(2)
    s = k_ref.shape[0]
    assert 3 * group <= V7X_LANES and group % ATTN_HEADS_PER_STEP == 0

    @pl.when(i == 0)
    def _():
        c = c_ref[...] * (-LOG2E)
        ct = jnp.concatenate([c, jnp.zeros((V7X_LANES - group, s), F32)], axis=0).T
        hi = ct.astype(BF16).astype(F32)
        r1 = ct - hi
        mid = r1.astype(BF16).astype(F32)
        lo = r1 - mid
        ext = hi + pltpu.roll(mid, group, 1) + pltpu.roll(lo, 2 * group, 1)
        kx_ref[:, :dh] = k_ref[...].astype(BF16)
        kx_ref[:, dh:] = ext.astype(BF16)
        vx_ref[:, :dh] = v_ref[...].astype(BF16)
        vx_ref[:, dh:] = jnp.ones((s, V7X_LANES), BF16)

    row = lax.broadcasted_iota(jnp.int32, (tq, tq), 0)
    col = lax.broadcasted_iota(jnp.int32, (tq, tq), 1)
    causal = col <= row
    lane = lax.broadcasted_iota(jnp.int32, (tq, V7X_LANES), 1)
    lane_head = jnp.where(lane < 3 * group, lax.rem(lane, group), -1)

    nh = ATTN_HEADS_PER_STEP

    def heads(gp, _):
        los, qcs = [], []
        for e in range(nh):
            gi = nh * gp + e
            lo_ = pl.multiple_of(gi * dh, dh)
            sel = jnp.where(lane_head == gi, 1.0, 0.0).astype(BF16)
            los.append(lo_)
            qcs.append(jnp.concatenate([q_ref[:, pl.ds(lo_, dh)], sel], axis=1))

        def blocks(off, carries, mask):
            kc = kx_ref[pl.ds(off, tq), :]
            vc = vx_ref[pl.ds(off, tq), :]
            return tuple(_flash_step(qcs[e], kc, vc, carries[e], mask=mask) for e in range(nh))

        init = _flash_init(tq, dh)
        carries = lax.fori_loop(0, i, lambda j, cs: blocks(pl.multiple_of(j * tq, tq), cs, None), (init,) * nh)
        carries = blocks(pl.multiple_of(i * tq, tq), carries, causal)
        for e in range(nh):
            o_ref[:, pl.ds(los[e], dh)] = _flash_out(carries[e], dh).astype(o_ref.dtype)
        return 0

    lax.fori_loop(0, group // nh, heads, 0)


def _fox_prompt_attn(q, kvz, ct, *, b, s, nkv, group, dh, tq):
    nq = s // tq
    return pl.pallas_call(
        functools.partial(_fox_pattn_kernel, tq=tq, dh=dh, group=group),
        grid=(b, nkv, nq),
        in_specs=[
            pl.BlockSpec((tq, group * dh), lambda bb, n, i: (bb * nq + i, n)),
            pl.BlockSpec((s, dh), lambda bb, n, i: (bb, n)),
            pl.BlockSpec((s, dh), lambda bb, n, i: (bb, nkv + n)),
            pl.BlockSpec((group, s), lambda bb, n, i: (n, bb)),
        ],
        out_specs=pl.BlockSpec((tq, group * dh), lambda bb, n, i: (bb * nq + i, n)),
        out_shape=jax.ShapeDtypeStruct((b * s, nkv * group * dh), BF16),
        scratch_shapes=[pltpu.VMEM((s, dh + V7X_LANES), BF16), pltpu.VMEM((s, dh + V7X_LANES), BF16)],
        compiler_params=_params(("parallel", "parallel", "arbitrary")),
        name="fox_prompt_attn",
    )(q, kvz, kvz, ct)


def _fox_rpast_kernel(pt_ref, lf_hbm, o_ref, lbuf, sem, *, layer, n_pages, page, blk):
    hh = o_ref.shape[1]
    streams = ((lf_hbm, layer, lbuf, sem, page, 1),)
    slot = _paged_prologue(streams, pt_ref, n_pages)

    n = n_pages * page
    nblk = n // blk
    lt = jnp.concatenate([lbuf[slot, :, k * blk:(k + 1) * blk] for k in range(nblk)], axis=0)
    r = _dot3(lt, _tri(blk, "suffix"))
    tot = jnp.sum(lt, axis=-1, keepdims=True)
    carry = jnp.zeros((hh, 1), F32)
    for k in range(nblk - 1, -1, -1):
        o_ref[0, :, k * blk:(k + 1) * blk] = r[k * hh:(k + 1) * hh, :] + carry
        carry = carry + tot[k * hh:(k + 1) * hh, :]


def _fox_rpast(page_table, cache_logf, *, layer, blk):
    db, n_pages = page_table.shape
    hh, page = cache_logf.shape[2], cache_logf.shape[3]
    n = n_pages * page
    grid_spec = pltpu.PrefetchScalarGridSpec(
        num_scalar_prefetch=1,
        grid=(db,),
        in_specs=[pl.BlockSpec(memory_space=pl.ANY)],
        out_specs=pl.BlockSpec((1, hh, n), lambda b, pt: (b, 0, 0)),
        scratch_shapes=[pltpu.VMEM((2, hh, n), F32), pltpu.SemaphoreType.DMA((2,))],
    )
    return pl.pallas_call(
        functools.partial(_fox_rpast_kernel, layer=layer, n_pages=n_pages, page=page, blk=blk),
        grid_spec=grid_spec,
        out_shape=jax.ShapeDtypeStruct((db, hh, n), F32),
        compiler_params=_params(("arbitrary",)),
        name="fox_rpast",
    )(page_table.reshape(-1), cache_logf)


def _div_const(x, d):
    if d & (d - 1) == 0:
        return lax.shift_right_logical(x, d.bit_length() - 1)
    return x // d


def _fox_dec_kernel(pt_ref, q_ref, kvn_ref, r_ref, cn_ref, k_hbm, v_hbm, o_ref, kbuf, vbuf, ksem, vsem,
                    *, layer, n_pages, page, tk, nkv, group, dh):
    rows = q_ref.shape[2]
    t = rows // group
    streams = ((k_hbm, layer, kbuf, ksem, page * nkv, 0), (v_hbm, layer, vbuf, vsem, page * nkv, 0))
    slot, nxt = _paged_begin(streams, pt_ref, n_pages)
    n_chunks = (n_pages * page) // tk
    ppc = n_pages // n_chunks
    assert ppc * n_chunks == n_pages

    qpos = _div_const(lax.broadcasted_iota(jnp.int32, (rows, V7X_LANES), 0), group)
    kpos = lax.broadcasted_iota(jnp.int32, (rows, V7X_LANES), 1)
    new_mask = kpos <= qpos
    qcs = [q_ref[0, n].astype(BF16) for n in range(nkv)]

    def with_ones(v):
        return jnp.concatenate([v, jnp.ones((v.shape[0], V7X_LANES), BF16)], axis=1)

    def tiled(b16):
        return jnp.concatenate([b16] * t, axis=0)

    def body(j, carries):
        _pages_issue(streams, pt_ref, nxt, 1 - slot, n_pages, j * ppc, ppc)
        off = pl.multiple_of(j * tk, tk)
        out = []
        for n in range(nkv):
            kc = kbuf[slot, pl.ds(off * nkv + n, tk, stride=nkv), :].astype(BF16)
            vc = vbuf[slot, pl.ds(off * nkv + n, tk, stride=nkv), :].astype(BF16)
            bias = tiled(r_ref[0, n * group:(n + 1) * group, pl.ds(off, tk)] * LOG2E)
            out.append(_flash_step(qcs[n], kc, with_ones(vc), carries[n], bias=bias))
        return tuple(out)

    init = _flash_init(rows, dh)
    carries = lax.fori_loop(0, n_chunks, body, (init,) * nkv)
    _paged_end(streams, slot)

    for n in range(nkv):
        kn = _pad_rows(kvn_ref[:, n * dh:(n + 1) * dh], V7X_LANES).astype(BF16)
        vn = _pad_rows(kvn_ref[:, (nkv + n) * dh:(nkv + n + 1) * dh], V7X_LANES).astype(BF16)
        bias = tiled(cn_ref[0, n * group:(n + 1) * group, :] * (-LOG2E))
        carry = _flash_step(qcs[n], kn, with_ones(vn), carries[n], mask=new_mask, bias=bias)
        o_ref[0, n] = _flash_out(carry, dh)


def _fox_decode(page_table, q_s, kvz, r_t, cn_b, cache_k, cache_v, *, layer, tp, tk, nkv, group, dh):
    db, n_pages = page_table.shape
    rows = q_s.shape[2]
    t = rows // group
    page = cache_k.shape[2] // nkv
    n = n_pages * page
    new_blk = tp // t
    hh = nkv * group
    grid_spec = pltpu.PrefetchScalarGridSpec(
        num_scalar_prefetch=1,
        grid=(db,),
        in_specs=[
            pl.BlockSpec((1, nkv, rows, dh), lambda b, pt: (b, 0, 0, 0)),
            pl.BlockSpec((t, kvz.shape[1]), lambda b, pt: (new_blk + b, 0)),
            pl.BlockSpec((1, hh, n), lambda b, pt: (b, 0, 0)),
            pl.BlockSpec((1, hh, V7X_LANES), lambda b, pt: (b, 0, 0)),
            pl.BlockSpec(memory_space=pl.ANY),
            pl.BlockSpec(memory_space=pl.ANY),
        ],
        out_specs=pl.BlockSpec((1, nkv, rows, dh), lambda b, pt: (b, 0, 0, 0)),
        scratch_shapes=[
            pltpu.VMEM((2, n * nkv, dh), F32),
            pltpu.VMEM((2, n * nkv, dh), F32),
            pltpu.SemaphoreType.DMA((2,)),
            pltpu.SemaphoreType.DMA((2,)),
        ],
    )
    return pl.pallas_call(
        functools.partial(_fox_dec_kernel, layer=layer, n_pages=n_pages, page=page, tk=tk, nkv=nkv, group=group, dh=dh),
        grid_spec=grid_spec,
        out_shape=jax.ShapeDtypeStruct(q_s.shape, F32),
        compiler_params=_params(("arbitrary",)),
        name="fox_decode",
    )(page_table.reshape(-1), q_s, kvz, r_t, cn_b, cache_k, cache_v)


def _rot_cols(w):
    half = w.shape[-1] // 2
    return jnp.concatenate([-w[..., half:], w[..., :half]], axis=-1)


def _mla_weights(w_dq, w_uq, w_dkv, w_uk, w_uv, *, c, dn, dr, pe_tile):
    d, ql = w_dq.shape
    h = w_uq.shape[1]
    w_c, w_pe = w_dkv[:, :c], w_dkv[:, c:]
    w_rot = _rot_cols(w_pe)
    pad = jnp.zeros((d, ql - (c + 4 * dr) % ql), w_dq.dtype) if (c + 4 * dr) % ql else jnp.zeros((d, 0), w_dq.dtype)
    w1 = jnp.concatenate([w_c, w_pe, w_pe, w_rot, w_rot, pad, w_dq], axis=1).astype(BF16)
    wq_nope = w_uq[:, :, :dn].reshape(ql, h * dn).astype(BF16)
    pe = w_uq[:, :, dn:]
    hpt = pe_tile // dr
    pe_t = pe.reshape(ql, h // hpt, hpt * dr)
    rot_t = _rot_cols(pe).reshape(ql, h // hpt, hpt * dr)
    wq_pe = jnp.concatenate([pe_t, rot_t], axis=2).reshape(ql, 2 * h * dr).astype(BF16)
    w_uk_flat = w_uk.reshape(c, h * dn).astype(BF16)
    w_uv_flat = w_uv.reshape(c, -1).astype(BF16)
    w_kv_up = jnp.concatenate([w_uk_flat, w_uv_flat], axis=1)
    return w1, wq_nope, wq_pe, w_uk_flat, w_uv_flat, w_kv_up


def kernel(x_prompt, x_sample, cache_mla_ckv, cache_mla_kpe, cache_fox_k, cache_fox_v, cache_fox_logf, page_table, norm_mixer, norm_mlp, norm_final, mla_w_dq, mla_g_q, mla_w_uq, mla_w_dkv, mla_g_kv, mla_w_uk, mla_w_uv, mla_w_o, fox_w_q, fox_w_k, fox_w_v, fox_w_f, fox_b_f, fox_w_o, mlp_w_up, mlp_w_down):
    b, s, d = x_prompt.shape
    db, ds, _ = x_sample.shape
    depth = norm_mixer.shape[0]
    n_pages = page_table.shape[1]
    page = cache_mla_ckv.shape[2]
    past_len = n_pages * page
    tp, ts = b * s, db * ds
    t = tp + ts

    ql = mla_w_dq.shape[2]
    c = mla_w_uk.shape[1]
    h = mla_w_uk.shape[2]
    dn = mla_w_uk.shape[3]
    dv = mla_w_uv.shape[3]
    dr = mla_w_uq.shape[3] - dn
    mla_scale = float(dn + dr) ** -0.5

    fh = fox_w_f.shape[2]
    nkv, dh = cache_fox_k.shape[3], cache_fox_k.shape[4]
    group = fh // nkv
    fox_scale = float(dh) ** -0.5

    tm = _row_tile(tp, ts, ROW_TILE)
    tn = N_TILE
    tq = _largest_tile(s, Q_TILE, V7X_LANES)
    tk_mla = _largest_tile(past_len, MLA_KV_CHUNK, page)
    tk_fox = _largest_tile(past_len, FOX_KV_CHUNK, page)
    tm_mlp = _largest_tile(t, MLP_ROW_TILE, V7X_BF16_SUBLANES)
    tf = _largest_tile(mlp_w_up.shape[2], MLP_F_TILE, V7X_LANES)
    pe_tile = 8 * dr
    np_tiles = tp // tm

    def col_tile(n, k):
        return _largest_tile(n, min(N_TILE_MAX, max(N_TILE, N_TILE_BYTES // (2 * k))), V7X_LANES)

    mlp_up_b = _tile_cast(mlp_w_up, tf)
    mlp_down_b = mlp_w_down.astype(BF16)

    x = jnp.concatenate([x_prompt.reshape(tp, d), x_sample.reshape(ts, d)], axis=0)

    half = dr // 2
    inv_freq = ROPE_THETA ** (-jnp.arange(half, dtype=F32) / half)
    pos = jnp.concatenate([jnp.tile(jnp.arange(s), b), jnp.tile(past_len + jnp.arange(ds), db)])
    ang = pos.astype(F32)[:, None] * inv_freq[None, :]
    cos_t = jnp.tile(jnp.cos(ang), (1, pe_tile // half))
    sin_t = jnp.tile(jnp.sin(ang), (1, pe_tile // half))

    cache_k2 = cache_fox_k.reshape(cache_fox_k.shape[:2] + (page * nkv, dh))
    cache_v2 = cache_fox_v.reshape(cache_fox_v.shape[:2] + (page * nkv, dh))
    cache_kpe_t = jnp.swapaxes(cache_mla_kpe, 2, 3)
    cache_logf_t = jnp.swapaxes(cache_fox_logf, 2, 3)

    outs = {k: [] for k in ("ckv", "kpe", "fk", "fv", "fl")}

    for i in range(depth):
        li = i // 2
        if i % 2 == 0:
            w1, wq_nope, wq_pe, w_uk_flat, w_uv_flat, w_kv_up = _mla_weights(
                mla_w_dq[li], mla_w_uq[li], mla_w_dkv[li], mla_w_uk[li], mla_w_uv[li], c=c, dn=dn, dr=dr, pe_tile=pe_tile)
            w_o = mla_w_o[li].astype(BF16)
            kvw = w1.shape[1] - ql
            proj = _mm(x, w1, rows=t, tm=tm, tn=col_tile(w1.shape[1], d), out_dtype=F32, g=norm_mixer[i],
                       name="mla_down")
            ckv, kpe_dup = _kv_post(proj, mla_g_kv[li], cos_t, sin_t, c=c, tm=tm)
            outs["ckv"].append(ckv)
            outs["kpe"].append(kpe_dup[:, :dr])
            q_nope = _mm(proj, wq_nope, rows=t, tm=tm, tn=col_tile(h * dn, ql), out_dtype=BF16, k_block=kvw // ql,
                         g=mla_g_q[li], scale=mla_scale * LOG2E, name="mla_q_nope")
            q_pe = _mm(proj, wq_pe, rows=t, tm=tm, tn=2 * pe_tile, out_dtype=BF16, k_block=kvw // ql, g=mla_g_q[li],
                       rope=(cos_t, sin_t), scale=mla_scale * LOG2E, name="mla_q_pe")
            knv = _mm(ckv, w_kv_up, rows=tp, tm=tm, tn=col_tile(w_kv_up.shape[1], c), out_dtype=BF16, name="mla_kv_up")
            attn_p = _mla_prompt_attn(q_nope, q_pe, knv, kpe_dup, b=b, s=s, h=h, dn=dn, dv=dv, tq=tq)
            q_lat = _head_absorb(q_nope, w_uk_flat, h=h, dn=dn, ts=ts, row_block=tp // ts)
            q_pe_s = q_pe[tp:].reshape(ts, h, dr).transpose(1, 0, 2).astype(F32)
            o_lat = _mla_decode(page_table, q_lat, q_pe_s, ckv, kpe_dup, cache_mla_ckv, cache_kpe_t,
                                layer=li, tp=tp, tk=tk_mla)
            attn_s = _head_value(o_lat, w_uv_flat, h=h, dv=dv)
        else:
            w_q = fox_w_q[li].astype(BF16)
            w_o = fox_w_o[li].astype(BF16)
            zpad = V7X_LANES - fh
            w_kvz = jnp.concatenate([fox_w_k[li], fox_w_v[li], fox_w_f[li], jnp.zeros((d, zpad), F32)], axis=1).astype(BF16)
            w_ft = fox_w_f[li].T.astype(BF16)
            b_row = jnp.concatenate([fox_b_f[li], jnp.zeros((zpad,), F32)]).reshape(1, V7X_LANES)
            b_col = fox_b_f[li].reshape(fh, 1)
            kvw = 2 * nkv * dh
            q = _mm(x, w_q, rows=t, tm=tm, tn=col_tile(w_q.shape[1], d), out_dtype=BF16, g=norm_mixer[i],
                    scale=fox_scale * LOG2E, name="fox_q")
            kvz, lft = _fox_kvz(x, norm_mixer[i], w_kvz, w_ft, b_row, b_col, kvw=kvw, tm=tm)
            outs["fk"].append(kvz[:, :nkv * dh])
            outs["fv"].append(kvz[:, nkv * dh:kvw])
            outs["fl"].append(kvz[:, kvw:kvw + fh])
            blk = 2 * V7X_LANES
            ct = _fox_cum(lft, width=s, n_blocks=b, block0=0, seg=0, blk=blk)
            attn_p = _fox_prompt_attn(q, kvz, ct, b=b, s=s, nkv=nkv, group=group, dh=dh, tq=tq)
            wn = _largest_tile(ts, blk, math.lcm(ds, V7X_LANES))
            cnt = _fox_cum(lft, width=wn, n_blocks=ts // wn, block0=tp // wn, seg=ds, blk=wn)
            cn_b = cnt.reshape(fh, db, ds).transpose(1, 0, 2)
            cn_b = jnp.concatenate([cn_b, jnp.zeros((db, fh, V7X_LANES - ds), F32)], axis=2)
            r_t = _fox_rpast(page_table, cache_logf_t, layer=li, blk=blk)
            q_s = q[tp:].astype(F32).reshape(db, ds, nkv, group, dh).transpose(0, 2, 1, 3, 4)
            o_s = _fox_decode(page_table, q_s.reshape(db, nkv, ds * group, dh), kvz, r_t, cn_b, cache_k2, cache_v2,
                              layer=li, tp=tp, tk=tk_fox, nkv=nkv, group=group, dh=dh)
            attn_s = o_s.reshape(db, nkv, ds, group, dh).transpose(0, 2, 1, 3, 4).reshape(ts, fh * dh).astype(BF16)
        x = _mm(attn_p, w_o, rows=t, tm=tm, tn=col_tile(d, w_o.shape[0]), out_dtype=F32, a2=attn_s,
                np_tiles=np_tiles, res=x, name="attn_out")
        x = _mlp(x, norm_mlp[i], mlp_up_b, mlp_down_b, layer=i, tm=tm_mlp, tf=tf)

    y_p = _final_norm(x, norm_final, row0=0, rows=tp, tm=tm)
    y_s = _final_norm(x, norm_final, row0=tp, rows=ts, tm=tm)

    def split(rows_list, tail):
        st = jnp.stack(rows_list)
        n_l = st.shape[0]
        return st[:, :tp].reshape((n_l, b, s) + tail), st[:, tp:].reshape((n_l, db, ds) + tail)

    p_ckv, s_ckv = split(outs["ckv"], (c,))
    p_kpe, s_kpe = split(outs["kpe"], (dr,))
    p_fk, s_fk = split(outs["fk"], (nkv, dh))
    p_fv, s_fv = split(outs["fv"], (nkv, dh))
    p_fl, s_fl = split(outs["fl"], (fh,))
    return (y_p.reshape(b, s, d), y_s.reshape(db, ds, d),
            p_ckv, p_kpe, p_fk, p_fv, p_fl, s_ckv, s_kpe, s_fk, s_fv, s_fl)
```

```python
import functools
import math

import jax
import jax.numpy as jnp
from jax import lax
from jax.experimental import pallas as pl
from jax.experimental.pallas import tpu as pltpu

F32 = jnp.float32
BF16 = jnp.bfloat16

RMS_EPS = 1e-6
ROPE_THETA = 10000.0
NEG_BIG = -1e30

V7X_LANES = 128
V7X_BF16_SUBLANES = 16
V7X_VMEM_LIMIT_BYTES = 56 * 1024 * 1024

NT_DIMS = (((1,), (1,)), ((), ()))

ROW_TILE = 512
N_TILE = 512
Q_TILE = 512
ATTN_HEADS_PER_STEP = 8
FOX_HEADS_PER_STEP = 16
MLA_KV_CHUNK = 4096
FOX_KV_CHUNK = 4096
MLP_ROW_TILE = 512
MLP_F_TILE = 512
PAGE_ISSUE_UNROLL = 8
N_TILE_BYTES = 8 * 1024 * 1024
N_TILE_MAX = 2048

LOG2E = 1.4426950408889634


def _params(semantics, vmem=V7X_VMEM_LIMIT_BYTES):
    return pltpu.CompilerParams(dimension_semantics=semantics, vmem_limit_bytes=vmem)


def _largest_tile(n, target, multiple):
    best = None
    for t in range(multiple, min(n, target) + 1, multiple):
        if n % t == 0:
            best = t
    assert best is not None, (n, target, multiple)
    return best


def _row_tile(tp, ts, target):
    return _largest_tile(math.gcd(tp, ts), target, V7X_BF16_SUBLANES)


def _rms(x, g, eps=RMS_EPS):
    r = lax.rsqrt(jnp.mean(x * x, axis=-1, keepdims=True) + eps)
    return (x * r) * g


def _log_sigmoid(x):
    return jnp.minimum(x, 0.0) - jnp.log1p(jnp.exp(-jnp.abs(x)))


def _split3(x):
    hi = x.astype(BF16)
    r1 = x - hi.astype(F32)
    mid = r1.astype(BF16)
    lo = (r1 - mid.astype(F32)).astype(BF16)
    return hi, mid, lo


def _dot3(x, m):
    hi, mid, lo = _split3(x)
    out = jnp.dot(lo, m, preferred_element_type=F32)
    out = out + jnp.dot(mid, m, preferred_element_type=F32)
    return out + jnp.dot(hi, m, preferred_element_type=F32)


def _mm_kernel(*refs, n_a, np_tiles, do_norm, has_res, has_rope, scale):
    a_refs = refs[:n_a]
    pos = n_a
    g_ref = None
    if do_norm:
        g_ref = refs[pos]
        pos += 1
    w_ref = refs[pos]
    pos += 1
    res_ref = cos_ref = sin_ref = None
    if has_res:
        res_ref = refs[pos]
        pos += 1
    if has_rope:
        cos_ref, sin_ref = refs[pos], refs[pos + 1]
        pos += 2
    o_ref, h_ref = refs[pos], refs[pos + 1]

    i = pl.program_id(0)

    def fill(a_ref):
        x = a_ref[...].astype(F32)
        if do_norm:
            x = _rms(x, g_ref[...])
        h_ref[...] = x.astype(BF16)

    @pl.when(pl.program_id(1) == 0)
    def _():
        if n_a == 1:
            fill(a_refs[0])
        else:
            @pl.when(i < np_tiles)
            def _():
                fill(a_refs[0])

            @pl.when(i >= np_tiles)
            def _():
                fill(a_refs[1])

    acc = jnp.dot(h_ref[...], w_ref[...], preferred_element_type=F32)
    if has_rope:
        half = acc.shape[1] // 2
        acc = acc[:, :half] * cos_ref[...] + acc[:, half:] * sin_ref[...]
    if scale != 1.0:
        acc = acc * scale
    if has_res:
        acc = res_ref[...] + acc
    o_ref[...] = acc.astype(o_ref.dtype)


def _mm(a, w, *, rows, tm, tn, out_dtype, k_block=0, a2=None, np_tiles=0, g=None, res=None,
        rope=None, scale=1.0, name="mm"):
    k, n = w.shape
    assert rows % tm == 0 and n % tn == 0
    n_out = n // 2 if rope is not None else n
    tn_out = tn // 2 if rope is not None else tn
    grid = (rows // tm, n // tn)

    in_specs, args = [], []
    if a2 is None:
        in_specs.append(pl.BlockSpec((tm, k), lambda i, j: (i, k_block)))
        args.append(a)
    else:
        last = np_tiles - 1
        in_specs.append(pl.BlockSpec((tm, k), lambda i, j: (jnp.minimum(i, last), k_block)))
        in_specs.append(pl.BlockSpec((tm, k), lambda i, j: (jnp.maximum(i - np_tiles, 0), k_block)))
        args += [a, a2]
    if g is not None:
        in_specs.append(pl.BlockSpec((1, k), lambda i, j: (0, 0)))
        args.append(g.reshape(1, k).astype(F32))
    in_specs.append(pl.BlockSpec((k, tn), lambda i, j: (0, j)))
    args.append(w)
    if res is not None:
        in_specs.append(pl.BlockSpec((tm, tn), lambda i, j: (i, j)))
        args.append(res)
    if rope is not None:
        for t in rope:
            in_specs.append(pl.BlockSpec((tm, tn_out), lambda i, j: (i, 0)))
            args.append(t)

    kern = functools.partial(_mm_kernel, n_a=1 if a2 is None else 2, np_tiles=np_tiles,
                             do_norm=g is not None, has_res=res is not None,
                             has_rope=rope is not None, scale=scale)
    return pl.pallas_call(
        kern,
        grid=grid,
        in_specs=in_specs,
        out_specs=pl.BlockSpec((tm, tn_out), lambda i, j: (i, j)),
        out_shape=jax.ShapeDtypeStruct((rows, n_out), out_dtype),
        scratch_shapes=[pltpu.VMEM((tm, k), BF16)],
        compiler_params=_params(("parallel", "arbitrary")),
        name=name,
    )(*args)


def _mlp_kernel(x_ref, g_ref, wu_ref, wd_ref, o_ref, h_ref):
    @pl.when(pl.program_id(1) == 0)
    def _():
        x = x_ref[...]
        h_ref[...] = _rms(x, g_ref[...]).astype(BF16)
        o_ref[...] = x

    u = jnp.dot(h_ref[...], wu_ref[...], preferred_element_type=F32)
    a = jnp.square(jnp.maximum(u, 0.0)).astype(BF16)
    o_ref[...] += jnp.dot(a, wd_ref[...], preferred_element_type=F32)


def _mlp(x, g, w_up, w_down, *, layer, tm, tf):
    t, d = x.shape
    f = w_down.shape[1]
    return pl.pallas_call(
        _mlp_kernel,
        grid=(t // tm, f // tf),
        in_specs=[
            pl.BlockSpec((tm, d), lambda i, j: (i, 0), pipeline_mode=pl.Buffered(1)),
            pl.BlockSpec((1, d), lambda i, j: (0, 0)),
            pl.BlockSpec((None, None, d, tf), lambda i, j: (layer, j, 0, 0)),
            pl.BlockSpec((None, tf, d), lambda i, j: (layer, j, 0)),
        ],
        out_specs=pl.BlockSpec((tm, d), lambda i, j: (i, 0)),
        out_shape=jax.ShapeDtypeStruct((t, d), F32),
        scratch_shapes=[pltpu.VMEM((tm, d), BF16)],
        compiler_params=_params(("parallel", "arbitrary")),
        name="mlp",
    )(x, g.reshape(1, d).astype(F32), w_up, w_down)


def _tile_cast_kernel(w_ref, o_ref):
    o_ref[...] = w_ref[...].astype(o_ref.dtype)


def _tile_cast(w, tf):
    n_l, d, f = w.shape
    return pl.pallas_call(
        _tile_cast_kernel,
        grid=(n_l, f // tf),
        in_specs=[pl.BlockSpec((None, d, tf), lambda l, j: (l, 0, j))],
        out_specs=pl.BlockSpec((None, None, d, tf), lambda l, j: (l, j, 0, 0)),
        out_shape=jax.ShapeDtypeStruct((n_l, f // tf, d, tf), BF16),
        compiler_params=_params(("parallel", "parallel")),
        name="mlp_up_tiles",
    )(w)


def _norm_kernel(x_ref, g_ref, o_ref):
    o_ref[...] = _rms(x_ref[...], g_ref[...])


def _final_norm(x, g, *, row0, rows, tm):
    d = x.shape[1]
    blk0 = row0 // tm
    return pl.pallas_call(
        _norm_kernel,
        grid=(rows // tm,),
        in_specs=[pl.BlockSpec((tm, d), lambda i: (blk0 + i, 0)), pl.BlockSpec((1, d), lambda i: (0, 0))],
        out_specs=pl.BlockSpec((tm, d), lambda i: (i, 0)),
        out_shape=jax.ShapeDtypeStruct((rows, d), F32),
        compiler_params=_params(("parallel",)),
        name="final_norm",
    )(x, g.reshape(1, d).astype(F32))


def _kv_post_kernel(p_ref, g_ref, cos_ref, sin_ref, ckv_ref, kpe_ref, *, c):
    ckv_ref[...] = _rms(p_ref[:, :c], g_ref[...])
    w = kpe_ref.shape[1]
    kpe_ref[...] = p_ref[:, c:c + w] * cos_ref[...] + p_ref[:, c + w:c + 2 * w] * sin_ref[...]


def _kv_post(proj, g_kv, cos, sin, *, c, tm):
    t = proj.shape[0]
    wb = c + 2 * V7X_LANES
    return pl.pallas_call(
        functools.partial(_kv_post_kernel, c=c),
        grid=(t // tm,),
        in_specs=[
            pl.BlockSpec((tm, wb), lambda i: (i, 0)),
            pl.BlockSpec((1, c), lambda i: (0, 0)),
            pl.BlockSpec((tm, V7X_LANES), lambda i: (i, 0)),
            pl.BlockSpec((tm, V7X_LANES), lambda i: (i, 0)),
        ],
        out_specs=[pl.BlockSpec((tm, c), lambda i: (i, 0)), pl.BlockSpec((tm, V7X_LANES), lambda i: (i, 0))],
        out_shape=[jax.ShapeDtypeStruct((t, c), F32), jax.ShapeDtypeStruct((t, V7X_LANES), F32)],
        compiler_params=_params(("parallel",)),
        name="mla_kv_post",
    )(proj, g_kv.reshape(1, c).astype(F32), cos, sin)


def _flash_step(qc, kc, vx, carry, mask=None, bias=None):
    m, acc = carry
    s = lax.dot_general(qc, kc, NT_DIMS, preferred_element_type=F32)
    if bias is not None:
        s = s + bias
    if mask is not None:
        s = jnp.where(mask, s, NEG_BIG)
    m_new = jnp.maximum(m, jnp.max(s, axis=-1, keepdims=True))
    p = jnp.exp2(s - m_new)
    acc = jnp.exp2(m - m_new) * acc + jnp.dot(p.astype(BF16), vx, preferred_element_type=F32)
    return m_new, acc


def _flash_init(rows, dv):
    return (jnp.full((rows, 1), NEG_BIG, F32), jnp.zeros((rows, dv + V7X_LANES), F32))


def _flash_out(carry, dv):
    _, acc = carry
    assert dv == V7X_LANES
    return acc[:, :dv] / acc[:, dv:]


def _mla_pattn_kernel(qn_ref, qp_ref, k_ref, v_ref, pe_ref, o_ref, kcat_ref, vx_ref, *, tq, dn, dv, nh):
    i = pl.program_id(2)
    s = k_ref.shape[0]
    half = V7X_LANES // 2

    @pl.when(i == 0)
    def _():
        pe = pe_ref[...].astype(BF16)
        ones = jnp.ones((s, V7X_LANES), BF16)
        for hh in range(nh):
            kcat_ref[hh, :, 0:dn] = k_ref[:, hh * dn:(hh + 1) * dn]
            kcat_ref[hh, :, dn:dn + V7X_LANES] = pe
            vx_ref[hh, :, 0:dv] = v_ref[:, hh * dv:(hh + 1) * dv]
            vx_ref[hh, :, dv:dv + V7X_LANES] = ones

    lane = lax.broadcasted_iota(jnp.int32, (tq, V7X_LANES), 1)
    row = lax.broadcasted_iota(jnp.int32, (tq, tq), 0)
    col = lax.broadcasted_iota(jnp.int32, (tq, tq), 1)
    causal = col <= row
    qcs = []
    for hh in range(nh):
        qp = qp_ref[:, (hh // 2) * V7X_LANES:(hh // 2 + 1) * V7X_LANES]
        keep = (lane >= half) if hh % 2 == 1 else (lane < half)
        qcs.append(jnp.concatenate([qn_ref[:, hh * dn:(hh + 1) * dn], jnp.where(keep, qp, jnp.zeros_like(qp))], axis=1))

    def blocks(off, carries, mask):
        return tuple(_flash_step(qcs[hh], kcat_ref[hh, pl.ds(off, tq), :], vx_ref[hh, pl.ds(off, tq), :],
                                 carries[hh], mask=mask) for hh in range(nh))

    init = _flash_init(tq, dv)
    carries = lax.fori_loop(0, i, lambda j, cs: blocks(pl.multiple_of(j * tq, tq), cs, None), (init,) * nh)
    carries = blocks(pl.multiple_of(i * tq, tq), carries, causal)
    for hh in range(nh):
        o_ref[:, hh * dv:(hh + 1) * dv] = _flash_out(carries[hh], dv).astype(o_ref.dtype)


def _mla_prompt_attn(q_nope, q_pe, knv, kpe_dup, *, b, s, h, dn, dv, tq):
    nq = s // tq
    nh = ATTN_HEADS_PER_STEP
    assert h % nh == 0 and nh % 2 == 0 and dn == dv
    dr = q_pe.shape[1] // h
    v_off = (h * dn) // (nh * dv)
    return pl.pallas_call(
        functools.partial(_mla_pattn_kernel, tq=tq, dn=dn, dv=dv, nh=nh),
        grid=(b, h // nh, nq),
        in_specs=[
            pl.BlockSpec((tq, nh * dn), lambda bb, p, i: (bb * nq + i, p)),
            pl.BlockSpec((tq, nh * dr), lambda bb, p, i: (bb * nq + i, p)),
            pl.BlockSpec((s, nh * dn), lambda bb, p, i: (bb, p)),
            pl.BlockSpec((s, nh * dv), lambda bb, p, i: (bb, v_off + p)),
            pl.BlockSpec((s, V7X_LANES), lambda bb, p, i: (bb, 0)),
        ],
        out_specs=pl.BlockSpec((tq, nh * dv), lambda bb, p, i: (bb * nq + i, p)),
        out_shape=jax.ShapeDtypeStruct((b * s, h * dv), BF16),
        scratch_shapes=[pltpu.VMEM((nh, s, dn + V7X_LANES), BF16), pltpu.VMEM((nh, s, dv + V7X_LANES), BF16)],
        compiler_params=_params(("parallel", "parallel", "arbitrary")),
        name="mla_prompt_attn",
    )(q_nope, q_pe, knv, knv, kpe_dup)


def _head_nt_kernel(a_ref, w_ref, o_ref):
    o_ref[0] = lax.dot_general(a_ref[...], w_ref[...], NT_DIMS, preferred_element_type=F32)


def _head_absorb(q_nope, w_uk_flat, *, h, dn, ts, row_block):
    c = w_uk_flat.shape[0]
    return pl.pallas_call(
        _head_nt_kernel,
        grid=(h,),
        in_specs=[pl.BlockSpec((ts, dn), lambda hh: (row_block, hh)), pl.BlockSpec((c, dn), lambda hh: (0, hh))],
        out_specs=pl.BlockSpec((1, ts, c), lambda hh: (hh, 0, 0)),
        out_shape=jax.ShapeDtypeStruct((h, ts, c), F32),
        compiler_params=_params(("parallel",)),
        name="mla_absorb_q",
    )(q_nope, w_uk_flat)


def _head_nn_kernel(a_ref, w_ref, o_ref):
    o_ref[...] = jnp.dot(a_ref[0].astype(BF16), w_ref[...], preferred_element_type=F32).astype(o_ref.dtype)


def _head_value(o_lat, w_uv_flat, *, h, dv):
    _, ts, c = o_lat.shape
    return pl.pallas_call(
        _head_nn_kernel,
        grid=(h,),
        in_specs=[pl.BlockSpec((1, ts, c), lambda hh: (hh, 0, 0)), pl.BlockSpec((c, dv), lambda hh: (0, hh))],
        out_specs=pl.BlockSpec((ts, dv), lambda hh: (0, hh)),
        out_shape=jax.ShapeDtypeStruct((ts, h * dv), BF16),
        compiler_params=_params(("parallel",)),
        name="mla_value_up",
    )(o_lat, w_uv_flat)


def _page_copy(stream, pt_ref, base, p, slot):
    hbm, layer, buf, sem, extent, axis = stream
    pg = pt_ref[base + p]
    span = pl.ds(pl.multiple_of(p * extent, extent), extent)
    dst = buf.at[slot, span, :] if axis == 0 else buf.at[slot, :, span]
    return pltpu.make_async_copy(hbm.at[layer, pg], dst, sem.at[slot])


def _pages_start(streams, pt_ref, bb, slot, n_pages):
    def body(p, c):
        for st in streams:
            _page_copy(st, pt_ref, bb * n_pages, p, slot).start()
        return c
    lax.fori_loop(0, n_pages, body, 0, unroll=math.gcd(n_pages, PAGE_ISSUE_UNROLL))


def _pages_wait(streams, slot):
    for _, _, buf, sem, _, _ in streams:
        pltpu.make_async_copy(buf.at[slot], buf.at[slot], sem.at[slot]).wait()


def _paged_prologue(streams, pt_ref, n_pages):
    b = pl.program_id(0)
    slot = lax.rem(b, 2)

    @pl.when(b == 0)
    def _():
        _pages_start(streams, pt_ref, b, slot, n_pages)

    @pl.when(b + 1 < pl.num_programs(0))
    def _():
        _pages_start(streams, pt_ref, b + 1, 1 - slot, n_pages)

    _pages_wait(streams, slot)
    return slot


def _paged_begin(streams, pt_ref, n_pages):
    b = pl.program_id(0)
    slot = lax.rem(b, 2)

    @pl.when(b == 0)
    def _():
        _pages_start(streams, pt_ref, b, slot, n_pages)

    _pages_wait(streams, slot)
    return slot, lax.rem(b + 1, pl.num_programs(0))


def _pages_issue(streams, pt_ref, bb, slot, n_pages, p0, count):
    for q in range(count):
        for st in streams:
            _page_copy(st, pt_ref, bb * n_pages, p0 + q, slot).start()


def _paged_end(streams, slot):
    @pl.when(pl.program_id(0) == pl.num_programs(0) - 1)
    def _():
        _pages_wait(streams, 1 - slot)


def _pad_rows(x, rows):
    return jnp.concatenate([x, jnp.zeros((rows - x.shape[0], x.shape[1]), x.dtype)], axis=0)


def _mla_dec_kernel(pt_ref, ql_ref, qp_ref, cn_ref, pn_ref, ckv_hbm, kpe_hbm, o_ref, kbuf, pbuf, ksem, psem,
                    *, layer, n_pages, page, tk, dr):
    h, t, c = ql_ref.shape
    rows = h * t
    streams = ((ckv_hbm, layer, kbuf, ksem, page, 0), (kpe_hbm, layer, pbuf, psem, page, 1))
    slot, nxt = _paged_begin(streams, pt_ref, n_pages)
    n_chunks = (n_pages * page) // tk
    ppc = n_pages // n_chunks
    assert ppc * n_chunks == n_pages

    ql = ql_ref[...].reshape(rows, c).astype(BF16)
    qp = qp_ref[...].reshape(rows, dr).astype(BF16)

    def step(kc, s_pe, carry, mask=None):
        m, l, acc = carry
        s = lax.dot_general(ql, kc, NT_DIMS, preferred_element_type=F32) + s_pe
        if mask is not None:
            s = jnp.where(mask, s, NEG_BIG)
        m_new = jnp.maximum(m, jnp.max(s, axis=-1, keepdims=True))
        alpha = jnp.exp2(m - m_new)
        p = jnp.exp2(s - m_new)
        l = alpha * l + jnp.sum(p, axis=-1, keepdims=True)
        acc = alpha * acc + jnp.dot(p.astype(BF16), kc, preferred_element_type=F32)
        return m_new, l, acc

    def body(j, carry):
        _pages_issue(streams, pt_ref, nxt, 1 - slot, n_pages, j * ppc, ppc)
        off = pl.multiple_of(j * tk, tk)
        kc = kbuf[slot, pl.ds(off, tk), :].astype(BF16)
        pct = pbuf[slot, :, pl.ds(off, tk)].astype(BF16)
        return step(kc, jnp.dot(qp, pct, preferred_element_type=F32), carry)

    init = (jnp.full((rows, 1), NEG_BIG, F32), jnp.zeros((rows, 1), F32), jnp.zeros((rows, c), F32))
    carry = lax.fori_loop(0, n_chunks, body, init)
    _paged_end(streams, slot)

    kn = _pad_rows(cn_ref[...], V7X_LANES).astype(BF16)
    pn = _pad_rows(pn_ref[:, :dr], V7X_LANES).astype(BF16)
    qpos = lax.rem(lax.broadcasted_iota(jnp.int32, (rows, V7X_LANES), 0), t)
    kpos = lax.broadcasted_iota(jnp.int32, (rows, V7X_LANES), 1)
    s_pe = lax.dot_general(qp, pn, NT_DIMS, preferred_element_type=F32)
    m, l, acc = step(kn, s_pe, carry, mask=kpos <= qpos)
    o_ref[...] = (acc / l).reshape(h, t, c)


def _mla_decode(page_table, q_lat, q_pe_s, ckv, kpe_dup, cache_ckv, cache_kpe, *, layer, tp, tk):
    db, n_pages = page_table.shape
    h, ts, c = q_lat.shape
    t = ts // db
    dr = q_pe_s.shape[2]
    page = cache_ckv.shape[2]
    new_blk = tp // t
    grid_spec = pltpu.PrefetchScalarGridSpec(
        num_scalar_prefetch=1,
        grid=(db,),
        in_specs=[
            pl.BlockSpec((h, t, c), lambda b, pt: (0, b, 0)),
            pl.BlockSpec((h, t, dr), lambda b, pt: (0, b, 0)),
            pl.BlockSpec((t, c), lambda b, pt: (new_blk + b, 0)),
            pl.BlockSpec((t, V7X_LANES), lambda b, pt: (new_blk + b, 0)),
            pl.BlockSpec(memory_space=pl.ANY),
            pl.BlockSpec(memory_space=pl.ANY),
        ],
        out_specs=pl.BlockSpec((h, t, c), lambda b, pt: (0, b, 0)),
        scratch_shapes=[
            pltpu.VMEM((2, n_pages * page, c), F32),
            pltpu.VMEM((2, dr, n_pages * page), F32),
            pltpu.SemaphoreType.DMA((2,)),
            pltpu.SemaphoreType.DMA((2,)),
        ],
    )
    return pl.pallas_call(
        functools.partial(_mla_dec_kernel, layer=layer, n_pages=n_pages, page=page, tk=tk, dr=dr),
        grid_spec=grid_spec,
        out_shape=jax.ShapeDtypeStruct((h, ts, c), F32),
        compiler_params=_params(("arbitrary",)),
        name="mla_decode",
    )(page_table.reshape(-1), q_lat, q_pe_s, ckv, kpe_dup, cache_ckv, cache_kpe)


def _fox_kvz_kernel(x_ref, g_ref, w_ref, wt_ref, b_ref, bt_ref, o_ref, ot_ref, *, kvw):
    hb = _rms(x_ref[...], g_ref[...]).astype(BF16)
    acc = jnp.dot(hb, w_ref[...], preferred_element_type=F32)
    o_ref[:, :kvw] = acc[:, :kvw]
    o_ref[:, kvw:] = _log_sigmoid(acc[:, kvw:] + b_ref[...])
    zt = lax.dot_general(wt_ref[...], hb, NT_DIMS, preferred_element_type=F32)
    ot_ref[...] = _log_sigmoid(zt + bt_ref[...])


def _fox_kvz(x, g, w_kvz, w_ft, b_row, b_col, *, kvw, tm):
    t, d = x.shape
    n = w_kvz.shape[1]
    hh = w_ft.shape[0]
    return pl.pallas_call(
        functools.partial(_fox_kvz_kernel, kvw=kvw),
        grid=(t // tm,),
        in_specs=[
            pl.BlockSpec((tm, d), lambda i: (i, 0)),
            pl.BlockSpec((1, d), lambda i: (0, 0)),
            pl.BlockSpec((d, n), lambda i: (0, 0)),
            pl.BlockSpec((hh, d), lambda i: (0, 0)),
            pl.BlockSpec((1, n - kvw), lambda i: (0, 0)),
            pl.BlockSpec((hh, 1), lambda i: (0, 0)),
        ],
        out_specs=[pl.BlockSpec((tm, n), lambda i: (i, 0)), pl.BlockSpec((hh, tm), lambda i: (0, i))],
        out_shape=[jax.ShapeDtypeStruct((t, n), F32), jax.ShapeDtypeStruct((hh, t), F32)],
        compiler_params=_params(("parallel",)),
        name="fox_kvz",
    )(x, g.reshape(1, d).astype(F32), w_kvz, w_ft, b_row, b_col)


def _tri(n, kind, seg=0):
    j = lax.broadcasted_iota(jnp.int32, (n, n), 0)
    t = lax.broadcasted_iota(jnp.int32, (n, n), 1)
    if kind == "incl":
        m = j <= t
    elif kind == "suffix":
        m = j > t
    if seg:
        m = jnp.logical_and(m, _div_const(j, seg) == _div_const(t, seg))
    return jnp.where(m, 1.0, 0.0).astype(BF16)


def _cum_kernel(x_ref, o_ref, *, blk, seg):
    hh, n = x_ref.shape
    tri = _tri(blk, "incl", seg)
    carry = jnp.zeros((hh, 1), F32)
    for k in range(n // blk):
        x = x_ref[:, k * blk:(k + 1) * blk]
        o_ref[:, k * blk:(k + 1) * blk] = _dot3(x, tri) + carry
        if not seg:
            carry = carry + jnp.sum(x, axis=-1, keepdims=True)


def _fox_cum(lft, *, width, n_blocks, block0, seg, blk):
    hh = lft.shape[0]
    return pl.pallas_call(
        functools.partial(_cum_kernel, blk=blk, seg=seg),
        grid=(n_blocks,),
        in_specs=[pl.BlockSpec((hh, width), lambda i: (0, block0 + i))],
        out_specs=pl.BlockSpec((hh, width), lambda i: (0, i)),
        out_shape=jax.ShapeDtypeStruct((hh, width * n_blocks), F32),
        compiler_params=_params(("parallel",)),
        name="fox_cumsum",
    )(lft)


def _fox_pattn_kernel(q_ref, k_ref, v_ref, c_ref, o_ref, kx_ref, vx_ref, *, tq, dh, group):
    i = pl.program_id(2)
    s = k_ref.shape[0]
    nh = math.gcd(group, FOX_HEADS_PER_STEP)
    assert 3 * group <= V7X_LANES

    @pl.when(i == 0)
    def _():
        c = c_ref[...] * (-LOG2E)
        ct = jnp.concatenate([c, jnp.zeros((V7X_LANES - group, s), F32)], axis=0).T
        hi = ct.astype(BF16).astype(F32)
        r1 = ct - hi
        mid = r1.astype(BF16).astype(F32)
        lo = r1 - mid
        ext = hi + pltpu.roll(mid, group, 1) + pltpu.roll(lo, 2 * group, 1)
        kx_ref[:, :dh] = k_ref[...].astype(BF16)
        kx_ref[:, dh:] = ext.astype(BF16)
        vx_ref[:, :dh] = v_ref[...].astype(BF16)
        vx_ref[:, dh:] = jnp.ones((s, V7X_LANES), BF16)

    row = lax.broadcasted_iota(jnp.int32, (tq, tq), 0)
    col = lax.broadcasted_iota(jnp.int32, (tq, tq), 1)
    causal = col <= row
    lane = lax.broadcasted_iota(jnp.int32, (tq, V7X_LANES), 1)
    lane_head = jnp.where(lane < 3 * group, lax.rem(lane, group), -1)

    def heads(gp, _):
        los, qcs = [], []
        for e in range(nh):
            gi = nh * gp + e
            lo_ = pl.multiple_of(gi * dh, dh)
            sel = jnp.where(lane_head == gi, 1.0, 0.0).astype(BF16)
            los.append(lo_)
            qcs.append(jnp.concatenate([q_ref[:, pl.ds(lo_, dh)], sel], axis=1))

        def blocks(off, carries, mask):
            kc = kx_ref[pl.ds(off, tq), :]
            vc = vx_ref[pl.ds(off, tq), :]
            return tuple(_flash_step(qcs[e], kc, vc, carries[e], mask=mask) for e in range(nh))

        init = _flash_init(tq, dh)
        carries = lax.fori_loop(0, i, lambda j, cs: blocks(pl.multiple_of(j * tq, tq), cs, None), (init,) * nh)
        carries = blocks(pl.multiple_of(i * tq, tq), carries, causal)
        for e in range(nh):
            o_ref[:, pl.ds(los[e], dh)] = _flash_out(carries[e], dh).astype(o_ref.dtype)
        return 0

    lax.fori_loop(0, group // nh, heads, 0)


def _fox_prompt_attn(q, kvz, ct, *, b, s, nkv, group, dh, tq):
    nq = s // tq
    return pl.pallas_call(
        functools.partial(_fox_pattn_kernel, tq=tq, dh=dh, group=group),
        grid=(b, nkv, nq),
        in_specs=[
            pl.BlockSpec((tq, group * dh), lambda bb, n, i: (bb * nq + i, n)),
            pl.BlockSpec((s, dh), lambda bb, n, i: (bb, n)),
            pl.BlockSpec((s, dh), lambda bb, n, i: (bb, nkv + n)),
            pl.BlockSpec((group, s), lambda bb, n, i: (n, bb)),
        ],
        out_specs=pl.BlockSpec((tq, group * dh), lambda bb, n, i: (bb * nq + i, n)),
        out_shape=jax.ShapeDtypeStruct((b * s, nkv * group * dh), BF16),
        scratch_shapes=[pltpu.VMEM((s, dh + V7X_LANES), BF16), pltpu.VMEM((s, dh + V7X_LANES), BF16)],
        compiler_params=_params(("parallel", "parallel", "arbitrary")),
        name="fox_prompt_attn",
    )(q, kvz, kvz, ct)


def _fox_rpast_kernel(pt_ref, lf_hbm, o_ref, lbuf, sem, *, layer, n_pages, page, blk):
    hh = o_ref.shape[1]
    streams = ((lf_hbm, layer, lbuf, sem, page, 1),)
    slot = _paged_prologue(streams, pt_ref, n_pages)

    n = n_pages * page
    nblk = n // blk
    lt = jnp.concatenate([lbuf[slot, :, k * blk:(k + 1) * blk] for k in range(nblk)], axis=0)
    r = _dot3(lt, _tri(blk, "suffix"))
    tot = jnp.sum(lt, axis=-1, keepdims=True)
    carry = jnp.zeros((hh, 1), F32)
    for k in range(nblk - 1, -1, -1):
        o_ref[0, :, k * blk:(k + 1) * blk] = r[k * hh:(k + 1) * hh, :] + carry
        carry = carry + tot[k * hh:(k + 1) * hh, :]


def _fox_rpast(page_table, cache_logf, *, layer, blk):
    db, n_pages = page_table.shape
    hh, page = cache_logf.shape[2], cache_logf.shape[3]
    n = n_pages * page
    grid_spec = pltpu.PrefetchScalarGridSpec(
        num_scalar_prefetch=1,
        grid=(db,),
        in_specs=[pl.BlockSpec(memory_space=pl.ANY)],
        out_specs=pl.BlockSpec((1, hh, n), lambda b, pt: (b, 0, 0)),
        scratch_shapes=[pltpu.VMEM((2, hh, n), F32), pltpu.SemaphoreType.DMA((2,))],
    )
    return pl.pallas_call(
        functools.partial(_fox_rpast_kernel, layer=layer, n_pages=n_pages, page=page, blk=blk),
        grid_spec=grid_spec,
        out_shape=jax.ShapeDtypeStruct((db, hh, n), F32),
        compiler_params=_params(("arbitrary",)),
        name="fox_rpast",
    )(page_table.reshape(-1), cache_logf)


def _div_const(x, d):
    if d & (d - 1) == 0:
        return lax.shift_right_logical(x, d.bit_length() - 1)
    return x // d


def _fox_dec_kernel(pt_ref, q_ref, kvn_ref, r_ref, cn_ref, k_hbm, v_hbm, o_ref, kbuf, vbuf, ksem, vsem,
                    *, layer, n_pages, page, tk, nkv, group, dh):
    rows = q_ref.shape[2]
    t = rows // group
    streams = ((k_hbm, layer, kbuf, ksem, page * nkv, 0), (v_hbm, layer, vbuf, vsem, page * nkv, 0))
    slot = _paged_prologue(streams, pt_ref, n_pages)
    n_chunks = (n_pages * page) // tk

    qpos = _div_const(lax.broadcasted_iota(jnp.int32, (rows, V7X_LANES), 0), group)
    kpos = lax.broadcasted_iota(jnp.int32, (rows, V7X_LANES), 1)
    new_mask = kpos <= qpos
    qcs = [q_ref[0, n].astype(BF16) for n in range(nkv)]

    def with_ones(v):
        return jnp.concatenate([v, jnp.ones((v.shape[0], V7X_LANES), BF16)], axis=1)

    def tiled(b16):
        return jnp.concatenate([b16] * t, axis=0)

    def body(j, carries):
        off = pl.multiple_of(j * tk, tk)
        out = []
        for n in range(nkv):
            kc = kbuf[slot, pl.ds(off * nkv + n, tk, stride=nkv), :].astype(BF16)
            vc = vbuf[slot, pl.ds(off * nkv + n, tk, stride=nkv), :].astype(BF16)
            bias = tiled(r_ref[0, n * group:(n + 1) * group, pl.ds(off, tk)] * LOG2E)
            out.append(_flash_step(qcs[n], kc, with_ones(vc), carries[n], bias=bias))
        return tuple(out)

    init = _flash_init(rows, dh)
    carries = lax.fori_loop(0, n_chunks, body, (init,) * nkv)

    for n in range(nkv):
        kn = _pad_rows(kvn_ref[:, n * dh:(n + 1) * dh], V7X_LANES).astype(BF16)
        vn = _pad_rows(kvn_ref[:, (nkv + n) * dh:(nkv + n + 1) * dh], V7X_LANES).astype(BF16)
        bias = tiled(cn_ref[0, n * group:(n + 1) * group, :] * (-LOG2E))
        carry = _flash_step(qcs[n], kn, with_ones(vn), carries[n], mask=new_mask, bias=bias)
        o_ref[0, n] = _flash_out(carry, dh)


def _fox_decode(page_table, q_s, kvz, r_t, cn_b, cache_k, cache_v, *, layer, tp, tk, nkv, group, dh):
    db, n_pages = page_table.shape
    rows = q_s.shape[2]
    t = rows // group
    page = cache_k.shape[2] // nkv
    n = n_pages * page
    new_blk = tp // t
    hh = nkv * group
    grid_spec = pltpu.PrefetchScalarGridSpec(
        num_scalar_prefetch=1,
        grid=(db,),
        in_specs=[
            pl.BlockSpec((1, nkv, rows, dh), lambda b, pt: (b, 0, 0, 0)),
            pl.BlockSpec((t, kvz.shape[1]), lambda b, pt: (new_blk + b, 0)),
            pl.BlockSpec((1, hh, n), lambda b, pt: (b, 0, 0)),
            pl.BlockSpec((1, hh, V7X_LANES), lambda b, pt: (b, 0, 0)),
            pl.BlockSpec(memory_space=pl.ANY),
            pl.BlockSpec(memory_space=pl.ANY),
        ],
        out_specs=pl.BlockSpec((1, nkv, rows, dh), lambda b, pt: (b, 0, 0, 0)),
        scratch_shapes=[
            pltpu.VMEM((2, n * nkv, dh), F32),
            pltpu.VMEM((2, n * nkv, dh), F32),
            pltpu.SemaphoreType.DMA((2,)),
            pltpu.SemaphoreType.DMA((2,)),
        ],
    )
    return pl.pallas_call(
        functools.partial(_fox_dec_kernel, layer=layer, n_pages=n_pages, page=page, tk=tk, nkv=nkv, group=group, dh=dh),
        grid_spec=grid_spec,
        out_shape=jax.ShapeDtypeStruct(q_s.shape, F32),
        compiler_params=_params(("arbitrary",)),
        name="fox_decode",
    )(page_table.reshape(-1), q_s, kvz, r_t, cn_b, cache_k, cache_v)


def _rot_cols(w):
    half = w.shape[-1] // 2
    return jnp.concatenate([-w[..., half:], w[..., :half]], axis=-1)


def _mla_weights(w_dq, w_uq, w_dkv, w_uk, w_uv, *, c, dn, dr, pe_tile):
    d, ql = w_dq.shape
    h = w_uq.shape[1]
    w_c, w_pe = w_dkv[:, :c], w_dkv[:, c:]
    w_rot = _rot_cols(w_pe)
    pad = jnp.zeros((d, ql - (c + 4 * dr) % ql), w_dq.dtype) if (c + 4 * dr) % ql else jnp.zeros((d, 0), w_dq.dtype)
    w1 = jnp.concatenate([w_c, w_pe, w_pe, w_rot, w_rot, pad, w_dq], axis=1).astype(BF16)
    wq_nope = w_uq[:, :, :dn].reshape(ql, h * dn).astype(BF16)
    pe = w_uq[:, :, dn:]
    hpt = pe_tile // dr
    pe_t = pe.reshape(ql, h // hpt, hpt * dr)
    rot_t = _rot_cols(pe).reshape(ql, h // hpt, hpt * dr)
    wq_pe = jnp.concatenate([pe_t, rot_t], axis=2).reshape(ql, 2 * h * dr).astype(BF16)
    w_uk_flat = w_uk.reshape(c, h * dn).astype(BF16)
    w_uv_flat = w_uv.reshape(c, -1).astype(BF16)
    w_kv_up = jnp.concatenate([w_uk_flat, w_uv_flat], axis=1)
    return w1, wq_nope, wq_pe, w_uk_flat, w_uv_flat, w_kv_up


def kernel(x_prompt, x_sample, cache_mla_ckv, cache_mla_kpe, cache_fox_k, cache_fox_v, cache_fox_logf, page_table, norm_mixer, norm_mlp, norm_final, mla_w_dq, mla_g_q, mla_w_uq, mla_w_dkv, mla_g_kv, mla_w_uk, mla_w_uv, mla_w_o, fox_w_q, fox_w_k, fox_w_v, fox_w_f, fox_b_f, fox_w_o, mlp_w_up, mlp_w_down):
    b, s, d = x_prompt.shape
    db, ds, _ = x_sample.shape
    depth = norm_mixer.shape[0]
    n_pages = page_table.shape[1]
    page = cache_mla_ckv.shape[2]
    past_len = n_pages * page
    tp, ts = b * s, db * ds
    t = tp + ts

    ql = mla_w_dq.shape[2]
    c = mla_w_uk.shape[1]
    h = mla_w_uk.shape[2]
    dn = mla_w_uk.shape[3]
    dv = mla_w_uv.shape[3]
    dr = mla_w_uq.shape[3] - dn
    mla_scale = float(dn + dr) ** -0.5

    fh = fox_w_f.shape[2]
    nkv, dh = cache_fox_k.shape[3], cache_fox_k.shape[4]
    group = fh // nkv
    fox_scale = float(dh) ** -0.5

    tm = _row_tile(tp, ts, ROW_TILE)
    tn = N_TILE
    tq = _largest_tile(s, Q_TILE, V7X_LANES)
    tk_mla = _largest_tile(past_len, MLA_KV_CHUNK, page)
    tk_fox = _largest_tile(past_len, FOX_KV_CHUNK, page)
    tm_mlp = _largest_tile(t, MLP_ROW_TILE, V7X_BF16_SUBLANES)
    tf = _largest_tile(mlp_w_up.shape[2], MLP_F_TILE, V7X_LANES)
    pe_tile = 8 * dr
    np_tiles = tp // tm

    def col_tile(n, k):
        return _largest_tile(n, min(N_TILE_MAX, max(N_TILE, N_TILE_BYTES // (2 * k))), V7X_LANES)

    mlp_up_b = _tile_cast(mlp_w_up, tf)
    mlp_down_b = mlp_w_down.astype(BF16)

    x = jnp.concatenate([x_prompt.reshape(tp, d), x_sample.reshape(ts, d)], axis=0)

    half = dr // 2
    inv_freq = ROPE_THETA ** (-jnp.arange(half, dtype=F32) / half)
    pos = jnp.concatenate([jnp.tile(jnp.arange(s), b), jnp.tile(past_len + jnp.arange(ds), db)])
    ang = pos.astype(F32)[:, None] * inv_freq[None, :]
    cos_t = jnp.tile(jnp.cos(ang), (1, pe_tile // half))
    sin_t = jnp.tile(jnp.sin(ang), (1, pe_tile // half))

    cache_k2 = cache_fox_k.reshape(cache_fox_k.shape[:2] + (page * nkv, dh))
    cache_v2 = cache_fox_v.reshape(cache_fox_v.shape[:2] + (page * nkv, dh))
    cache_kpe_t = jnp.swapaxes(cache_mla_kpe, 2, 3)
    cache_logf_t = jnp.swapaxes(cache_fox_logf, 2, 3)

    outs = {k: [] for k in ("ckv", "kpe", "fk", "fv", "fl")}

    for i in range(depth):
        li = i // 2
        if i % 2 == 0:
            w1, wq_nope, wq_pe, w_uk_flat, w_uv_flat, w_kv_up = _mla_weights(
                mla_w_dq[li], mla_w_uq[li], mla_w_dkv[li], mla_w_uk[li], mla_w_uv[li], c=c, dn=dn, dr=dr, pe_tile=pe_tile)
            w_o = mla_w_o[li].astype(BF16)
            kvw = w1.shape[1] - ql
            proj = _mm(x, w1, rows=t, tm=tm, tn=col_tile(w1.shape[1], d), out_dtype=F32, g=norm_mixer[i],
                       name="mla_down")
            ckv, kpe_dup = _kv_post(proj, mla_g_kv[li], cos_t, sin_t, c=c, tm=tm)
            outs["ckv"].append(ckv)
            outs["kpe"].append(kpe_dup[:, :dr])
            q_nope = _mm(proj, wq_nope, rows=t, tm=tm, tn=col_tile(h * dn, ql), out_dtype=BF16, k_block=kvw // ql,
                         g=mla_g_q[li], scale=mla_scale * LOG2E, name="mla_q_nope")
            q_pe = _mm(proj, wq_pe, rows=t, tm=tm, tn=2 * pe_tile, out_dtype=BF16, k_block=kvw // ql, g=mla_g_q[li],
                       rope=(cos_t, sin_t), scale=mla_scale * LOG2E, name="mla_q_pe")
            knv = _mm(ckv, w_kv_up, rows=tp, tm=tm, tn=col_tile(w_kv_up.shape[1], c), out_dtype=BF16, name="mla_kv_up")
            attn_p = _mla_prompt_attn(q_nope, q_pe, knv, kpe_dup, b=b, s=s, h=h, dn=dn, dv=dv, tq=tq)
            q_lat = _head_absorb(q_nope, w_uk_flat, h=h, dn=dn, ts=ts, row_block=tp // ts)
            q_pe_s = q_pe[tp:].reshape(ts, h, dr).transpose(1, 0, 2).astype(F32)
            o_lat = _mla_decode(page_table, q_lat, q_pe_s, ckv, kpe_dup, cache_mla_ckv, cache_kpe_t,
                                layer=li, tp=tp, tk=tk_mla)
            attn_s = _head_value(o_lat, w_uv_flat, h=h, dv=dv)
        else:
            w_q = fox_w_q[li].astype(BF16)
            w_o = fox_w_o[li].astype(BF16)
            zpad = V7X_LANES - fh
            w_kvz = jnp.concatenate([fox_w_k[li], fox_w_v[li], fox_w_f[li], jnp.zeros((d, zpad), F32)], axis=1).astype(BF16)
            w_ft = fox_w_f[li].T.astype(BF16)
            b_row = jnp.concatenate([fox_b_f[li], jnp.zeros((zpad,), F32)]).reshape(1, V7X_LANES)
            b_col = fox_b_f[li].reshape(fh, 1)
            kvw = 2 * nkv * dh
            q = _mm(x, w_q, rows=t, tm=tm, tn=col_tile(w_q.shape[1], d), out_dtype=BF16, g=norm_mixer[i],
                    scale=fox_scale * LOG2E, name="fox_q")
            kvz, lft = _fox_kvz(x, norm_mixer[i], w_kvz, w_ft, b_row, b_col, kvw=kvw, tm=tm)
            outs["fk"].append(kvz[:, :nkv * dh])
            outs["fv"].append(kvz[:, nkv * dh:kvw])
            outs["fl"].append(kvz[:, kvw:kvw + fh])
            blk = 2 * V7X_LANES
            ct = _fox_cum(lft, width=s, n_blocks=b, block0=0, seg=0, blk=blk)
            attn_p = _fox_prompt_attn(q, kvz, ct, b=b, s=s, nkv=nkv, group=group, dh=dh, tq=tq)
            wn = _largest_tile(ts, blk, math.lcm(ds, V7X_LANES))
            cnt = _fox_cum(lft, width=wn, n_blocks=ts // wn, block0=tp // wn, seg=ds, blk=wn)
            cn_b = cnt.reshape(fh, db, ds).transpose(1, 0, 2)
            cn_b = jnp.concatenate([cn_b, jnp.zeros((db, fh, V7X_LANES - ds), F32)], axis=2)
            r_t = _fox_rpast(page_table, cache_logf_t, layer=li, blk=blk)
            q_s = q[tp:].astype(F32).reshape(db, ds, nkv, group, dh).transpose(0, 2, 1, 3, 4)
            o_s = _fox_decode(page_table, q_s.reshape(db, nkv, ds * group, dh), kvz, r_t, cn_b, cache_k2, cache_v2,
                              layer=li, tp=tp, tk=tk_fox, nkv=nkv, group=group, dh=dh)
            attn_s = o_s.reshape(db, nkv, ds, group, dh).transpose(0, 2, 1, 3, 4).reshape(ts, fh * dh).astype(BF16)
        x = _mm(attn_p, w_o, rows=t, tm=tm, tn=col_tile(d, w_o.shape[0]), out_dtype=F32, a2=attn_s,
                np_tiles=np_tiles, res=x, name="attn_out")
        x = _mlp(x, norm_mlp[i], mlp_up_b, mlp_down_b, layer=i, tm=tm_mlp, tf=tf)

    y_p = _final_norm(x, norm_final, row0=0, rows=tp, tm=tm)
    y_s = _final_norm(x, norm_final, row0=tp, rows=ts, tm=tm)

    def split(rows_list, tail):
        st = jnp.stack(rows_list)
        n_l = st.shape[0]
        return st[:, :tp].reshape((n_l, b, s) + tail), st[:, tp:].reshape((n_l, db, ds) + tail)

    p_ckv, s_ckv = split(outs["ckv"], (c,))
    p_kpe, s_kpe = split(outs["kpe"], (dr,))
    p_fk, s_fk = split(outs["fk"], (nkv, dh))
    p_fv, s_fv = split(outs["fv"], (nkv, dh))
    p_fl, s_fl = split(outs["fl"], (fh,))
    return (y_p.reshape(b, s, d), y_s.reshape(db, ds, d),
            p_ckv, p_kpe, p_fk, p_fv, p_fl, s_ckv, s_kpe, s_fk, s_fv, s_fl)
```
